```python
import jax, jax.numpy as jnp
from jax import lax
import numpy as np

D_MODEL = 1024
BATCH = 4
SEQ = 8192
DEPTH = 1

HEAD_DIM = 64
ROT_DIM = HEAD_DIM // 4
ROPE_THETA = 500000.0
BLOCK = 128
N_MEM = 256
D_MIX = D_MODEL
C_HEADS = 4
C_W = C_HEADS * HEAD_DIM
A_Q_HEADS = (D_MIX - C_W) // (2 * HEAD_DIM)
A_KV_HEADS = 2
A_GROUP = A_Q_HEADS // A_KV_HEADS
A_W = A_Q_HEADS * HEAD_DIM
A_KV_W = A_KV_HEADS * HEAD_DIM
A_WINDOW = 128
B_HEADS = A_Q_HEADS
B_W = B_HEADS * HEAD_DIM
B_CONFIGS = ((128, 1), (512, 4), (2048, 16))
RMS_EPS = 1e-6
IN_WIDTHS = (A_W, A_KV_W, A_KV_W, A_W,
             B_W, B_W, B_W, B_W,
             C_W, C_W)
D_IN = sum(IN_WIDTHS)
IN_SPLITS = tuple(int(s) for s in np.cumsum(IN_WIDTHS)[:-1])

kernel_name = "hybrid_swa_sink_dilated_memxattn_layer"


def rms_norm(x, g):
    xf = x.astype(jnp.float32)
    y = xf * lax.rsqrt(jnp.mean(xf * xf, axis=-1, keepdims=True) + RMS_EPS)
    return (y * g.astype(jnp.float32)).astype(x.dtype)


def rope_tables(seq):
    inv_freq = ROPE_THETA ** (-jnp.arange(0, ROT_DIM, 2, dtype=jnp.float32) / ROT_DIM)
    ang = jnp.arange(seq, dtype=jnp.float32)[:, None] * inv_freq[None, :]
    return jnp.cos(ang)[:, None, :], jnp.sin(ang)[:, None, :]


def apply_partial_rope(t, cos, sin):
    tf = t.astype(jnp.float32)
    half = ROT_DIM // 2
    r1, r2, rest = tf[..., :half], tf[..., half:ROT_DIM], tf[..., ROT_DIM:]
    out = jnp.concatenate([r1 * cos - r2 * sin, r2 * cos + r1 * sin, rest], axis=-1)
    return out.astype(t.dtype)


def banded_attention(q, k, v, max_dist, sink):
    N, L, KVH, G, Dh = q.shape
    nb = -(-L // BLOCK)
    Lp = nb * BLOCK
    pad = Lp - L
    n_prev = -(-max_dist // BLOCK)
    W = (n_prev + 1) * BLOCK
    qp = jnp.pad(q, ((0, 0), (0, pad), (0, 0), (0, 0), (0, 0)))
    kp = jnp.pad(k, ((0, 0), (n_prev * BLOCK, pad), (0, 0), (0, 0)))
    vp = jnp.pad(v, ((0, 0), (n_prev * BLOCK, pad), (0, 0), (0, 0)))

    def windows(t):
        return jnp.concatenate(
            [t[:, j * BLOCK:(j + nb) * BLOCK].reshape(N, nb, BLOCK, KVH, Dh)
             for j in range(n_prev + 1)], axis=2)

    kw, vw = windows(kp), windows(vp)
    qb = qp.reshape(N, nb, BLOCK, KVH, G, Dh)
    s = jnp.einsum('nbqhgd,nbkhd->nbhgqk', qb, kw,
                   preferred_element_type=jnp.float32)
    qi = jnp.arange(BLOCK)[:, None]
    kj = jnp.arange(W)[None, :]
    dist = qi + n_prev * BLOCK - kj
    kpos = jnp.arange(nb)[:, None, None] * BLOCK + kj[None] - n_prev * BLOCK
    mask = (dist >= 0)[None] & (dist <= max_dist)[None] & (kpos >= 0)
    s = jnp.where(mask[:, None, None], s, -jnp.inf)
    m = jnp.max(s, axis=-1)
    if sink is not None:
        sink_b = sink.astype(jnp.float32)[None, None, :, :, None]
        m = jnp.maximum(m, sink_b)
    p = jnp.exp(s - m[..., None])
    denom = jnp.sum(p, axis=-1)
    if sink is not None:
        denom = denom + jnp.exp(sink_b - m)
    o = jnp.einsum('nbhgqk,nbkhd->nbqhgd', p, vw.astype(jnp.float32))
    o = o / jnp.moveaxis(denom, -1, 2)[..., None]
    lse = jnp.moveaxis(m + jnp.log(denom), -1, 2)
    o = o.reshape(N, Lp, KVH, G, Dh)[:, :L]
    lse = lse.reshape(N, Lp, KVH, G)[:, :L]
    return o.astype(q.dtype), lse


def to_strided(t, d):
    B, S = t.shape[:2]
    rest = t.shape[2:]
    t = t.reshape((B, S // d, d) + rest)
    t = jnp.moveaxis(t, 2, 1)
    return t.reshape((B * d, S // d) + rest)


def from_strided(t, B, d):
    L = t.shape[1]
    rest = t.shape[2:]
    t = t.reshape((B, d, L) + rest)
    t = jnp.moveaxis(t, 1, 2)
    return t.reshape((B, d * L) + rest)


def setup_inputs(seed: int = 0) -> dict:
    key = jax.random.key(seed)
    ks = jax.random.split(key, 10)
    f32 = jnp.float32
    x = jax.random.normal(ks[0], (BATCH, SEQ, D_MODEL), f32)
    mem = jax.random.normal(ks[1], (BATCH, N_MEM, D_MODEL), f32)
    pre_norm = 1.0 + 0.02 * jax.random.normal(ks[2], (DEPTH, D_MODEL), f32)
    w_in = jax.random.normal(ks[3], (DEPTH, D_MODEL, D_IN), f32) * D_MODEL ** -0.5
    sink_a = 0.5 * jax.random.normal(ks[4], (DEPTH, A_Q_HEADS), f32)
    mem_norm = 1.0 + 0.02 * jax.random.normal(ks[5], (DEPTH, D_MODEL), f32)
    w_mem_kv = jax.random.normal(ks[6], (DEPTH, D_MODEL, 2 * C_W), f32) * D_MODEL ** -0.5
    w_out = jax.random.normal(ks[7], (DEPTH, D_MIX, D_MODEL), f32) * D_MIX ** -0.5
    post_norm = 1.0 + 0.02 * jax.random.normal(ks[8], (DEPTH, D_MODEL), f32)
    return {"x": x, "mem": mem, "pre_norm": pre_norm, "w_in": w_in,
            "sink_a": sink_a, "mem_norm": mem_norm, "w_mem_kv": w_mem_kv,
            "w_out": w_out, "post_norm": post_norm}


def reference(x, mem, pre_norm, w_in, sink_a, mem_norm, w_mem_kv, w_out, post_norm):
    B, S, _ = x.shape
    scale = HEAD_DIM ** -0.5
    cos, sin = rope_tables(S)
    h = x
    for l in range(DEPTH):
        u = rms_norm(h, pre_norm[l])
        proj = jnp.einsum('bsd,de->bse', u, w_in[l])
        qa, ka, va, ga, qb, kb, vb, gb, qc, gc = jnp.split(proj, IN_SPLITS, axis=-1)

        qa = apply_partial_rope(qa.reshape(B, S, A_Q_HEADS, HEAD_DIM), cos, sin)
        qa = qa.reshape(B, S, A_KV_HEADS, A_GROUP, HEAD_DIM)
        ka = apply_partial_rope(ka.reshape(B, S, A_KV_HEADS, HEAD_DIM), cos, sin)
        va = va.reshape(B, S, A_KV_HEADS, HEAD_DIM)
        sink = sink_a[l].reshape(A_KV_HEADS, A_GROUP)
        oa, _ = banded_attention(qa * scale, ka, va, A_WINDOW - 1, sink)
        oa = oa.reshape(B, S, A_W)

        qb = apply_partial_rope(qb.reshape(B, S, B_HEADS, HEAD_DIM), cos, sin)
        kb = apply_partial_rope(kb.reshape(B, S, B_HEADS, HEAD_DIM), cos, sin)
        vb = vb.reshape(B, S, B_HEADS, HEAD_DIM)
        outs, lses = [], []
        for (win, dil) in B_CONFIGS:
            qs = to_strided(qb * scale, dil)[:, :, :, None, :]
            o, lse = banded_attention(qs, to_strided(kb, dil), to_strided(vb, dil),
                                      win // dil, None)
            outs.append(from_strided(o[:, :, :, 0], B, dil).astype(jnp.float32))
            lses.append(from_strided(lse[..., 0], B, dil))
        wts = jax.nn.softmax(jnp.stack(lses, axis=0), axis=0)
        ob = jnp.sum(wts[..., None] * jnp.stack(outs, axis=0), axis=0)
        ob = ob.astype(x.dtype).reshape(B, S, B_W)

        mkv = jnp.einsum('bmd,de->bme', rms_norm(mem, mem_norm[l]), w_mem_kv[l])
        mk, mv = jnp.split(mkv, 2, axis=-1)
        mk = mk.reshape(B, N_MEM, C_HEADS, HEAD_DIM)
        mv = mv.reshape(B, N_MEM, C_HEADS, HEAD_DIM)
        qc = qc.reshape(B, S, C_HEADS, HEAD_DIM)
        sc = jnp.einsum('bshd,bmhd->bhsm', qc * scale, mk,
                        preferred_element_type=jnp.float32)
        pc = jax.nn.softmax(sc, axis=-1)
        oc = jnp.einsum('bhsm,bmhd->bshd', pc, mv.astype(jnp.float32))
        oc = oc.astype(x.dtype).reshape(B, S, C_W)

        y = jnp.concatenate([oa * jax.nn.silu(ga), ob * jax.nn.silu(gb),
                             oc * jax.nn.silu(gc)], axis=-1)
        y = jnp.einsum('bse,ed->bsd', y, w_out[l])
        h = h + rms_norm(y, post_norm[l])
    return h
```

```python
import functools

import jax
import jax.numpy as jnp
import numpy as np
from jax import lax
from jax.experimental import pallas as pl
from jax.experimental.pallas import tpu as pltpu

HEAD_DIM = 64
ROT_DIM = HEAD_DIM // 4
ROPE_THETA = 500000.0
BLOCK = 128
C_HEADS = 4
A_Q_HEADS = 6
A_KV_HEADS = 2
A_WINDOW = 128
B_HEADS = 6
B_CONFIGS = ((128, 1), (512, 4), (2048, 16))
RMS_EPS = 1e-6

LANES = 128
A_W = A_Q_HEADS * HEAD_DIM
A_KV_W = A_KV_HEADS * HEAD_DIM
B_W = B_HEADS * HEAD_DIM
C_W = C_HEADS * HEAD_DIM

A_HEAD_ORDER = (0, 3, 1, 4, 2, 5)

PROJ_ROWS = 512
ATTN_ROWS = 512
VMEM_LIMIT = 48 * 1024 * 1024


def _rms_scale(xf):
    return lax.rsqrt(jnp.mean(xf * xf, axis=-1, keepdims=True) + RMS_EPS)


def _silu(g):
    return g * (1.0 / (1.0 + jnp.exp(-g)))


def _mem_kv_kernel(mem_ref, g_ref, w_ref, mk_ref, mv_ref):
    m = mem_ref[0]
    u = (m * _rms_scale(m) * g_ref[...]).astype(jnp.bfloat16)
    kv = jnp.dot(u, w_ref[...], preferred_element_type=jnp.float32)
    mk_ref[0] = kv[:, :C_W].astype(jnp.bfloat16)
    mv_ref[0] = kv[:, C_W:].astype(jnp.bfloat16)


def _mem_kv(mem, g, w):
    b, n_mem, d = mem.shape
    return pl.pallas_call(
        _mem_kv_kernel,
        grid=(b,),
        in_specs=[
            pl.BlockSpec((1, n_mem, d), lambda i: (i, 0, 0)),
            pl.BlockSpec((1, d), lambda i: (0, 0)),
            pl.BlockSpec((d, 2 * C_W), lambda i: (0, 0)),
        ],
        out_specs=[
            pl.BlockSpec((1, n_mem, C_W), lambda i: (i, 0, 0)),
            pl.BlockSpec((1, n_mem, C_W), lambda i: (i, 0, 0)),
        ],
        out_shape=[jax.ShapeDtypeStruct((b, n_mem, C_W), jnp.bfloat16)] * 2,
        name="mem_kv",
    )(mem, g, w)


_COLS = {}
_off = 0
for _name, _w in (("qa", A_W), ("ka", A_KV_W), ("va", A_KV_W), ("qb", B_W), ("kb", B_W),
                  ("vb", B_W), ("qc", C_W), ("ga", A_W), ("gb", B_W), ("gc", C_W)):
    _COLS[_name] = (_off, _off + _w)
    _off += _w
D_IN = _off


def _rope(t, cos_t, sin_t, first8):
    half = ROT_DIM // 2
    out = []
    for j in range(t.shape[1] // LANES):
        tj = t[:, j * LANES:(j + 1) * LANES]
        partner = jnp.where(first8, pltpu.roll(tj, LANES - half, 1), pltpu.roll(tj, half, 1))
        out.append(tj * cos_t + partner * sin_t)
    return out[0] if len(out) == 1 else jnp.concatenate(out, axis=1)


def _half_select(left):
    lane = lax.broadcasted_iota(jnp.int32, (1, LANES), 1)
    return (lane < HEAD_DIM) if left else (lane >= HEAD_DIM)


def _in_proj_kernel(x_ref, g_ref, w_ref, cos_ref, sin_ref, mk_ref, mv_ref,
                    qa_ref, ka_ref, va_ref, qb_ref, kb_ref, vb_ref, ga_ref, gb_ref, yc_ref):
    scale = HEAD_DIM ** -0.5
    xf = x_ref[0]
    u = (xf * _rms_scale(xf) * g_ref[...]).astype(jnp.bfloat16)
    cos_t = cos_ref[...]
    sin_t = sin_ref[...]
    lane = lax.broadcasted_iota(jnp.int32, (1, LANES), 1)
    first8 = (lane % HEAD_DIM) < (ROT_DIM // 2)

    def proj(name):
        c0, c1 = _COLS[name]
        return jnp.dot(u, w_ref[:, c0:c1], preferred_element_type=jnp.float32)

    qa_ref[0] = (_rope(proj("qa"), cos_t, sin_t, first8) * scale).astype(jnp.bfloat16)
    ka_ref[0] = _rope(proj("ka"), cos_t, sin_t, first8).astype(jnp.bfloat16)
    va_ref[0] = proj("va").astype(jnp.bfloat16)
    qb_ref[0] = (_rope(proj("qb"), cos_t, sin_t, first8) * scale).astype(jnp.bfloat16)
    kb_ref[0] = _rope(proj("kb"), cos_t, sin_t, first8).astype(jnp.bfloat16)
    vb_ref[0] = proj("vb").astype(jnp.bfloat16)
    ga_ref[0] = _silu(proj("ga")).astype(jnp.bfloat16)
    gb_ref[0] = _silu(proj("gb")).astype(jnp.bfloat16)

    qc = (proj("qc") * scale).astype(jnp.bfloat16)
    gc = _silu(proj("gc"))
    left = _half_select(True)
    rows = qc.shape[0]
    oc = []
    for p in range(C_W // LANES):
        q2 = qc[:, p * LANES:(p + 1) * LANES]
        mk = mk_ref[0, :, p * LANES:(p + 1) * LANES]
        mv = mv_ref[0, :, p * LANES:(p + 1) * LANES]
        zero = jnp.zeros_like(q2)
        qs = jnp.concatenate([jnp.where(left, q2, zero), jnp.where(left, zero, q2)], axis=0)
        s = lax.dot_general(qs, mk, (((1,), (1,)), ((), ())), preferred_element_type=jnp.float32)
        m = jnp.max(s, axis=-1, keepdims=True)
        e = jnp.exp(s - m)
        l = jnp.sum(e, axis=-1, keepdims=True)
        pv = jnp.dot(e.astype(jnp.bfloat16), mv, preferred_element_type=jnp.float32) * (1.0 / l)
        oc.append(jnp.where(left, pv[:rows], pv[rows:]))
    yc_ref[0] = (jnp.concatenate(oc, axis=1) * gc).astype(jnp.bfloat16)


def _in_proj(x, g, w, cos_t, sin_t, mk, mv):
    b, s, d = x.shape
    tm = PROJ_ROWS
    n_mem = mk.shape[1]
    row = lambda width: pl.BlockSpec((1, tm, width), lambda i, bi: (bi, i, 0))
    out_widths = (A_W, A_KV_W, A_KV_W, B_W, B_W, B_W, A_W, B_W, C_W)
    return pl.pallas_call(
        _in_proj_kernel,
        grid=(s // tm, b),
        in_specs=[
            row(d),
            pl.BlockSpec((1, d), lambda i, bi: (0, 0)),
            pl.BlockSpec((d, D_IN), lambda i, bi: (0, 0)),
            pl.BlockSpec((tm, LANES), lambda i, bi: (i, 0)),
            pl.BlockSpec((tm, LANES), lambda i, bi: (i, 0)),
            pl.BlockSpec((1, n_mem, C_W), lambda i, bi: (bi, 0, 0)),
            pl.BlockSpec((1, n_mem, C_W), lambda i, bi: (bi, 0, 0)),
        ],
        out_specs=[row(wd) for wd in out_widths],
        out_shape=[jax.ShapeDtypeStruct((b, s, wd), jnp.bfloat16) for wd in out_widths],
        compiler_params=pltpu.CompilerParams(
            dimension_semantics=("parallel", "parallel"), vmem_limit_bytes=VMEM_LIMIT),
        name="in_proj",
    )(x, g, w, cos_t, sin_t, mk, mv)


def _band_attn_kernel(*refs, max_dist, kv_groups, has_sink, want_lse):
    refs = list(refs)
    sink_ref = refs.pop(0) if has_sink else None
    q_ref, kp_ref, ko_ref, vp_ref, vo_ref = refs[:5]
    o_ref = refs[5]
    lse_ref = refs[6] if want_lse else None
    ks_ref, vs_ref = refs[-2:]

    rows = q_ref.shape[1]
    first_step = pl.program_id(2) == 0

    ks_ref[0:BLOCK] = kp_ref[0]
    ks_ref[BLOCK:] = ko_ref[0]
    vs_ref[0:BLOCK] = vp_ref[0]
    vs_ref[BLOCK:] = vo_ref[0]

    qi = lax.broadcasted_iota(jnp.int32, (BLOCK, 2 * BLOCK), 0)
    kc = lax.broadcasted_iota(jnp.int32, (BLOCK, 2 * BLOCK), 1)
    dist = qi + BLOCK - kc
    band = (dist >= 0) & (dist <= max_dist)
    left = _half_select(True)

    def sub_block(i, carry):
        r0 = pl.multiple_of(i * BLOCK, BLOCK)
        q = q_ref[0, pl.ds(r0, BLOCK), :]
        k = ks_ref[pl.ds(r0, 2 * BLOCK), :]
        v = vs_ref[pl.ds(r0, 2 * BLOCK), :]
        first_key = jnp.where(jnp.logical_and(first_step, i == 0), BLOCK, 0)
        mask = band & (kc >= first_key)
        mask2 = jnp.concatenate([mask, mask], axis=0)
        for p in range(q.shape[1] // LANES):
            g = p if kv_groups > 1 else 0
            q2 = q[:, p * LANES:(p + 1) * LANES]
            kg = k[:, g * LANES:(g + 1) * LANES]
            vg = v[:, g * LANES:(g + 1) * LANES]
            zero = jnp.zeros_like(q2)
            qs = jnp.concatenate([jnp.where(left, q2, zero), jnp.where(left, zero, q2)], axis=0)
            s = lax.dot_general(qs, kg, (((1,), (1,)), ((), ())),
                                preferred_element_type=jnp.float32)
            s = jnp.where(mask2, s, -jnp.inf)
            m = jnp.max(s, axis=-1, keepdims=True)
            if has_sink:
                sink = jnp.concatenate(
                    [jnp.full((BLOCK, 1), sink_ref[p], jnp.float32),
                     jnp.full((BLOCK, 1), sink_ref[p + A_Q_HEADS // 2], jnp.float32)], axis=0)
                m = jnp.maximum(m, sink)
            e = jnp.exp(s - m)
            l = jnp.sum(e, axis=-1, keepdims=True)
            if has_sink:
                l = l + jnp.exp(sink - m)
            pv = jnp.dot(e.astype(jnp.bfloat16), vg, preferred_element_type=jnp.float32)
            pv = pv * (1.0 / l)
            o_ref[0, pl.ds(r0, BLOCK), p * LANES:(p + 1) * LANES] = jnp.where(
                left, pv[:BLOCK], pv[BLOCK:]).astype(o_ref.dtype)
            if want_lse:
                lse = m + jnp.log(l)
                lse_ref[0, pl.ds(r0, BLOCK), p * LANES:(p + 1) * LANES] = jnp.where(
                    left, lse[:BLOCK], lse[BLOCK:])
        return carry

    lax.fori_loop(0, rows // BLOCK, sub_block, 0)


def _band_attn(q, k, v, *, dil, max_dist, sink=None, want_lse=False):
    b, s, wq = q.shape
    wk = k.shape[2]
    seq = s // dil
    rows = min(ATTN_ROWS, seq)
    sub = rows // BLOCK
    qv = q.reshape(b, seq, dil * wq)
    kv = k.reshape(b, seq, dil * wk)
    vv = v.reshape(b, seq, dil * wk)
    own = lambda width, n: pl.BlockSpec((1, n, width), lambda bi, r, j: (bi, j, r))
    prev = lambda width: pl.BlockSpec(
        (1, BLOCK, width), lambda bi, r, j: (bi, jnp.maximum(j * sub - 1, 0), r))
    in_specs = [own(wq, rows), prev(wk), own(wk, rows), prev(wk), own(wk, rows)]
    args = [qv, kv, kv, vv, vv]
    if sink is not None:
        in_specs.insert(0, pl.BlockSpec(memory_space=pltpu.SMEM))
        args.insert(0, sink)
    out_specs = [own(wq, rows)]
    out_shape = [jax.ShapeDtypeStruct((b, seq, dil * wq), jnp.bfloat16)]
    if want_lse:
        out_specs.append(own(wq, rows))
        out_shape.append(jax.ShapeDtypeStruct((b, seq, dil * wq), jnp.float32))
    outs = pl.pallas_call(
        functools.partial(_band_attn_kernel, max_dist=max_dist, kv_groups=wk // LANES,
                          has_sink=sink is not None, want_lse=want_lse),
        grid=(b, dil, seq // rows),
        in_specs=in_specs,
        out_specs=out_specs,
        out_shape=out_shape,
        scratch_shapes=[pltpu.VMEM((rows + BLOCK, wk), jnp.bfloat16),
                        pltpu.VMEM((rows + BLOCK, wk), jnp.bfloat16)],
        compiler_params=pltpu.CompilerParams(
            dimension_semantics=("parallel", "parallel", "parallel"),
            vmem_limit_bytes=VMEM_LIMIT),
        name=f"band_attn_d{dil}",
    )(*args)
    return [o.reshape(b, s, wq) for o in outs]


def _out_proj_kernel(x_ref, oa_ref, o1_ref, o4_ref, o16_ref, l1_ref, l4_ref, l16_ref,
                     ga_ref, gb_ref, yc_ref, w_ref, g_ref, out_ref):
    f32 = jnp.float32
    l1, l4, l16 = l1_ref[0], l4_ref[0], l16_ref[0]
    mx = jnp.maximum(jnp.maximum(l1, l4), l16)
    e1, e4, e16 = jnp.exp(l1 - mx), jnp.exp(l4 - mx), jnp.exp(l16 - mx)
    den = e1 + e4 + e16
    ob = (e1 * o1_ref[0].astype(f32) + e4 * o4_ref[0].astype(f32)
          + e16 * o16_ref[0].astype(f32)) * (1.0 / den)
    ya = oa_ref[0].astype(f32) * ga_ref[0].astype(f32)
    yb = ob * gb_ref[0].astype(f32)
    y = jnp.concatenate([ya.astype(jnp.bfloat16), yb.astype(jnp.bfloat16), yc_ref[0]], axis=1)
    z = jnp.dot(y, w_ref[...], preferred_element_type=f32)
    out_ref[0] = x_ref[0] + z * _rms_scale(z) * g_ref[...]


def _out_proj(x, oa, o1, o4, o16, l1, l4, l16, ga, gb, yc, w, g):
    b, s, d = x.shape
    tm = PROJ_ROWS
    row = lambda width: pl.BlockSpec((1, tm, width), lambda bi, i: (bi, i, 0))
    return pl.pallas_call(
        _out_proj_kernel,
        grid=(b, s // tm),
        in_specs=[row(d), row(A_W)] + [row(B_W)] * 6 + [row(A_W), row(B_W), row(C_W),
                  pl.BlockSpec((d, d), lambda bi, i: (0, 0)),
                  pl.BlockSpec((1, d), lambda bi, i: (0, 0))],
        out_specs=row(d),
        out_shape=jax.ShapeDtypeStruct((b, s, d), x.dtype),
        compiler_params=pltpu.CompilerParams(
            dimension_semantics=("parallel", "parallel"), vmem_limit_bytes=VMEM_LIMIT),
        name="out_proj",
    )(x, oa, o1, o4, o16, l1, l4, l16, ga, gb, yc, w, g)


def _in_perm():
    ref_off = {}
    off = 0
    for name, wd in (("qa", A_W), ("ka", A_KV_W), ("va", A_KV_W), ("ga", A_W), ("qb", B_W),
                     ("kb", B_W), ("vb", B_W), ("gb", B_W), ("qc", C_W), ("gc", C_W)):
        ref_off[name] = off
        off += wd
    head_perm = np.concatenate([np.arange(h * HEAD_DIM, (h + 1) * HEAD_DIM) for h in A_HEAD_ORDER])
    cols = []
    for name, (c0, c1) in _COLS.items():
        idx = np.arange(c1 - c0)
        if name in ("qa", "ga"):
            idx = head_perm
        cols.append(ref_off[name] + idx)
    return np.concatenate(cols), head_perm


def _rope_tables(seq):
    inv_freq = ROPE_THETA ** (-jnp.arange(0, ROT_DIM, 2, dtype=jnp.float32) / ROT_DIM)
    ang = jnp.arange(seq, dtype=jnp.float32)[:, None] * inv_freq[None, :]
    cos, sin = jnp.cos(ang), jnp.sin(ang)
    pad = HEAD_DIM - ROT_DIM
    cos_h = jnp.concatenate([cos, cos, jnp.ones((seq, pad), jnp.float32)], axis=1)
    sin_h = jnp.concatenate([-sin, sin, jnp.zeros((seq, pad), jnp.float32)], axis=1)
    reps = LANES // HEAD_DIM
    return jnp.tile(cos_h, (1, reps)), jnp.tile(sin_h, (1, reps))


def kernel(x, mem, pre_norm, w_in, sink_a, mem_norm, w_mem_kv, w_out, post_norm):
    b, s, d = x.shape
    depth = pre_norm.shape[0]
    in_perm, head_perm = _in_perm()
    out_perm = np.concatenate([head_perm, np.arange(A_W, d)])
    cos_t, sin_t = _rope_tables(s)
    h = x
    for l in range(depth):
        w_in_k = w_in[l][:, in_perm].astype(jnp.bfloat16)
        w_out_k = w_out[l][out_perm, :].astype(jnp.bfloat16)
        mk, mv = _mem_kv(mem, mem_norm[l][None], w_mem_kv[l].astype(jnp.bfloat16))
        qa, ka, va, qb, kb, vb, ga, gb, yc = _in_proj(
            h, pre_norm[l][None], w_in_k, cos_t, sin_t, mk, mv)
        (oa,) = _band_attn(qa, ka, va, dil=1, max_dist=A_WINDOW - 1, sink=sink_a[l])
        ob = [_band_attn(qb, kb, vb, dil=dil, max_dist=win // dil, want_lse=True)
              for win, dil in B_CONFIGS]
        h = _out_proj(h, oa, ob[0][0], ob[1][0], ob[2][0], ob[0][1], ob[1][1], ob[2][1],
                      ga, gb, yc, w_out_k, post_norm[l][None])
    return h
```

```python
import functools
import math

import jax
import jax.numpy as jnp
import numpy as np
from jax import lax
from jax.experimental import pallas as pl
from jax.experimental.pallas import tpu as pltpu

HEAD_DIM = 64
ROT_DIM = HEAD_DIM // 4
ROPE_THETA = 500000.0
BLOCK = 128
C_HEADS = 4
A_Q_HEADS = 6
A_KV_HEADS = 2
A_WINDOW = 128
B_HEADS = 6
B_CONFIGS = ((128, 1), (512, 4), (2048, 16))
RMS_EPS = 1e-6

LANES = 128
A_W = A_Q_HEADS * HEAD_DIM
A_KV_W = A_KV_HEADS * HEAD_DIM
B_W = B_HEADS * HEAD_DIM
C_W = C_HEADS * HEAD_DIM

A_HEAD_ORDER = (0, 3, 1, 4, 2, 5)

PROJ_ROWS = 512
ATTN_ROWS = 512
VMEM_LIMIT = 48 * 1024 * 1024

LOG2E = math.log2(math.e)
LN2 = math.log(2.0)
Q_SCALE = HEAD_DIM ** -0.5 * LOG2E


def _rms_scale(xf):
    return lax.rsqrt(jnp.mean(xf * xf, axis=-1, keepdims=True) + RMS_EPS)


def _silu(g):
    return g * (1.0 / (1.0 + jnp.exp(-g)))


def _half_select(left):
    lane = lax.broadcasted_iota(jnp.int32, (1, LANES), 1)
    return (lane < HEAD_DIM) if left else (lane >= HEAD_DIM)


def _pair_attention(q2, kg, v_ones, bias, sink2):
    rows = q2.shape[0]
    left = _half_select(True)
    zero = jnp.zeros_like(q2)
    qs = jnp.concatenate([jnp.where(left, q2, zero), jnp.where(left, zero, q2)], axis=0)
    s = lax.dot_general(qs, kg, (((1,), (1,)), ((), ())), preferred_element_type=jnp.float32)
    if bias is not None:
        s = s + bias
    m = jnp.max(s, axis=-1, keepdims=True)
    if sink2 is not None:
        m = jnp.maximum(m, sink2)
    e = jnp.exp2(s - m).astype(jnp.bfloat16)
    pv = jnp.dot(e, v_ones, preferred_element_type=jnp.float32)
    l = pv[:, LANES:]
    if sink2 is not None:
        l = l + jnp.exp2(sink2 - m)
    o = pv[:, :LANES] * (1.0 / l)
    lse2 = m + jnp.log(l) * LOG2E
    return (jnp.where(left, o[:rows], o[rows:]), jnp.where(left, lse2[:rows], lse2[rows:]))


def _mem_kv_kernel(mem_ref, g_ref, w_ref, mk_ref, mv_ref):
    m = mem_ref[0]
    u = (m * _rms_scale(m) * g_ref[...]).astype(jnp.bfloat16)
    kv = jnp.dot(u, w_ref[...], preferred_element_type=jnp.float32)
    mk_ref[0] = kv[:, :C_W].astype(jnp.bfloat16)
    mv_ref[0] = kv[:, C_W:].astype(jnp.bfloat16)


def _mem_kv(mem, g, w):
    b, n_mem, d = mem.shape
    return pl.pallas_call(
        _mem_kv_kernel,
        grid=(b,),
        in_specs=[
            pl.BlockSpec((1, n_mem, d), lambda i: (i, 0, 0)),
            pl.BlockSpec((1, d), lambda i: (0, 0)),
            pl.BlockSpec((d, 2 * C_W), lambda i: (0, 0)),
        ],
        out_specs=[
            pl.BlockSpec((1, n_mem, C_W), lambda i: (i, 0, 0)),
            pl.BlockSpec((1, n_mem, C_W), lambda i: (i, 0, 0)),
        ],
        out_shape=[jax.ShapeDtypeStruct((b, n_mem, C_W), jnp.bfloat16)] * 2,
        name="mem_kv",
    )(mem, g, w)


_COLS = {}
_off = 0
for _name, _w in (("qa", A_W), ("ka", A_KV_W), ("va", A_KV_W), ("qb", B_W), ("kb", B_W),
                  ("vb", B_W), ("qc", C_W), ("ga", A_W), ("gb", B_W), ("gc", C_W)):
    _COLS[_name] = (_off, _off + _w)
    _off += _w
D_IN = _off


def _rope(t, cos_t, sin_t, first8):
    half = ROT_DIM // 2
    out = []
    for j in range(t.shape[1] // LANES):
        tj = t[:, j * LANES:(j + 1) * LANES]
        partner = jnp.where(first8, pltpu.roll(tj, LANES - half, 1), pltpu.roll(tj, half, 1))
        out.append(tj * cos_t + partner * sin_t)
    return out[0] if len(out) == 1 else jnp.concatenate(out, axis=1)


def _in_proj_kernel(x_ref, g_ref, w_ref, cos_ref, sin_ref, mk_ref, mv_ref,
                    qa_ref, ka_ref, va_ref, qb_ref, kb_ref, vb_ref, ga_ref, gb_ref, yc_ref):
    xf = x_ref[0]
    u = (xf * _rms_scale(xf) * g_ref[...]).astype(jnp.bfloat16)
    cos_t = cos_ref[...]
    sin_t = sin_ref[...]
    lane = lax.broadcasted_iota(jnp.int32, (1, LANES), 1)
    first8 = (lane % HEAD_DIM) < (ROT_DIM // 2)

    def proj(name):
        c0, c1 = _COLS[name]
        return jnp.dot(u, w_ref[:, c0:c1], preferred_element_type=jnp.float32)

    qa_ref[0] = (_rope(proj("qa"), cos_t, sin_t, first8) * Q_SCALE).astype(jnp.bfloat16)
    ka_ref[0] = _rope(proj("ka"), cos_t, sin_t, first8).astype(jnp.bfloat16)
    va_ref[0] = proj("va").astype(jnp.bfloat16)
    qb_ref[0] = (_rope(proj("qb"), cos_t, sin_t, first8) * Q_SCALE).astype(jnp.bfloat16)
    kb_ref[0] = _rope(proj("kb"), cos_t, sin_t, first8).astype(jnp.bfloat16)
    vb_ref[0] = proj("vb").astype(jnp.bfloat16)
    ga_ref[0] = _silu(proj("ga")).astype(jnp.bfloat16)
    gb_ref[0] = _silu(proj("gb")).astype(jnp.bfloat16)

    qc = (proj("qc") * Q_SCALE).astype(jnp.bfloat16)
    gc = _silu(proj("gc"))
    ones = jnp.ones((mv_ref.shape[1], LANES), jnp.bfloat16)
    oc = []
    for p in range(C_W // LANES):
        cols = slice(p * LANES, (p + 1) * LANES)
        v_ones = jnp.concatenate([mv_ref[0, :, cols], ones], axis=1)
        o, _ = _pair_attention(qc[:, cols], mk_ref[0, :, cols], v_ones, None, None)
        oc.append(o)
    yc_ref[0] = (jnp.concatenate(oc, axis=1) * gc).astype(jnp.bfloat16)


def _in_proj(x, g, w, cos_t, sin_t, mk, mv):
    b, s, d = x.shape
    tm = PROJ_ROWS
    n_mem = mk.shape[1]
    row = lambda width: pl.BlockSpec((1, tm, width), lambda i, bi: (bi, i, 0))
    out_widths = (A_W, A_KV_W, A_KV_W, B_W, B_W, B_W, A_W, B_W, C_W)
    return pl.pallas_call(
        _in_proj_kernel,
        grid=(s // tm, b),
        in_specs=[
            row(d),
            pl.BlockSpec((1, d), lambda i, bi: (0, 0)),
            pl.BlockSpec((d, D_IN), lambda i, bi: (0, 0)),
            pl.BlockSpec((tm, LANES), lambda i, bi: (i, 0)),
            pl.BlockSpec((tm, LANES), lambda i, bi: (i, 0)),
            pl.BlockSpec((1, n_mem, C_W), lambda i, bi: (bi, 0, 0)),
            pl.BlockSpec((1, n_mem, C_W), lambda i, bi: (bi, 0, 0)),
        ],
        out_specs=[row(wd) for wd in out_widths],
        out_shape=[jax.ShapeDtypeStruct((b, s, wd), jnp.bfloat16) for wd in out_widths],
        compiler_params=pltpu.CompilerParams(
            dimension_semantics=("parallel", "parallel"), vmem_limit_bytes=VMEM_LIMIT),
        name="in_proj",
    )(x, g, w, cos_t, sin_t, mk, mv)


def _band_attn_kernel(*refs, max_dist, kv_groups, has_sink, want_lse):
    refs = list(refs)
    sink_ref = refs.pop(0) if has_sink else None
    q_ref, kp_ref, ko_ref, vp_ref, vo_ref, o_ref = refs[:6]
    lse_ref = refs[6] if want_lse else None
    rows = q_ref.shape[1]

    qi = lax.broadcasted_iota(jnp.int32, (BLOCK, 2 * BLOCK), 0)
    kc = lax.broadcasted_iota(jnp.int32, (BLOCK, 2 * BLOCK), 1)
    dist = qi + BLOCK - kc
    band = (dist >= 0) & (dist <= max_dist)
    first_key = jnp.where(pl.program_id(2) == 0, BLOCK, 0)
    neg = jnp.float32(-jnp.inf)
    bias = jnp.where(band, 0.0, neg)
    bias_first = jnp.where(band & (kc >= first_key), 0.0, neg)
    bias = jnp.concatenate([bias, bias], axis=0)
    bias_first = jnp.concatenate([bias_first, bias_first], axis=0)
    ones = jnp.ones((2 * BLOCK, LANES), jnp.bfloat16)

    for i in range(rows // BLOCK):
        r = slice(i * BLOCK, (i + 1) * BLOCK)
        q = q_ref[0, r, :]
        if i == 0:
            k = jnp.concatenate([kp_ref[0], ko_ref[0, r, :]], axis=0)
            v = jnp.concatenate([vp_ref[0], vo_ref[0, r, :]], axis=0)
        else:
            k = ko_ref[0, (i - 1) * BLOCK:(i + 1) * BLOCK, :]
            v = vo_ref[0, (i - 1) * BLOCK:(i + 1) * BLOCK, :]
        for p in range(q.shape[1] // LANES):
            g = p if kv_groups > 1 else 0
            cols = slice(p * LANES, (p + 1) * LANES)
            kcols = slice(g * LANES, (g + 1) * LANES)
            sink2 = None
            if has_sink:
                sink2 = LOG2E * jnp.concatenate(
                    [jnp.full((BLOCK, 1), sink_ref[p], jnp.float32),
                     jnp.full((BLOCK, 1), sink_ref[p + A_Q_HEADS // 2], jnp.float32)], axis=0)
            v_ones = jnp.concatenate([v[:, kcols], ones], axis=1)
            o, lse2 = _pair_attention(q[:, cols], k[:, kcols], v_ones,
                                      bias_first if i == 0 else bias, sink2)
            o_ref[0, r, cols] = o.astype(o_ref.dtype)
            if want_lse:
                lse_ref[0, r, cols] = lse2


def _band_attn(q, k, v, *, dil, max_dist, sink=None, want_lse=False):
    b, s, wq = q.shape
    wk = k.shape[2]
    seq = s // dil
    rows = min(ATTN_ROWS, seq)
    sub = rows // BLOCK
    qv = q.reshape(b, seq, dil * wq)
    kv = k.reshape(b, seq, dil * wk)
    vv = v.reshape(b, seq, dil * wk)
    own = lambda width, n: pl.BlockSpec((1, n, width), lambda bi, r, j: (bi, j, r))
    prev = lambda width: pl.BlockSpec(
        (1, BLOCK, width), lambda bi, r, j: (bi, jnp.maximum(j * sub - 1, 0), r))
    in_specs = [own(wq, rows), prev(wk), own(wk, rows), prev(wk), own(wk, rows)]
    args = [qv, kv, kv, vv, vv]
    if sink is not None:
        in_specs.insert(0, pl.BlockSpec(memory_space=pltpu.SMEM))
        args.insert(0, sink)
    out_specs = [own(wq, rows)]
    out_shape = [jax.ShapeDtypeStruct((b, seq, dil * wq), jnp.bfloat16)]
    if want_lse:
        out_specs.append(own(wq, rows))
        out_shape.append(jax.ShapeDtypeStruct((b, seq, dil * wq), jnp.float32))
    outs = pl.pallas_call(
        functools.partial(_band_attn_kernel, max_dist=max_dist, kv_groups=wk // LANES,
                          has_sink=sink is not None, want_lse=want_lse),
        grid=(b, dil, seq // rows),
        in_specs=in_specs,
        out_specs=out_specs,
        out_shape=out_shape,
        compiler_params=pltpu.CompilerParams(
            dimension_semantics=("parallel", "parallel", "parallel"),
            vmem_limit_bytes=VMEM_LIMIT),
        name=f"band_attn_d{dil}",
    )(*args)
    return [o.reshape(b, s, wq) for o in outs]


def _out_proj_kernel(x_ref, oa_ref, o1_ref, o4_ref, o16_ref, l1_ref, l4_ref, l16_ref,
                     ga_ref, gb_ref, yc_ref, w_ref, g_ref, out_ref):
    f32 = jnp.float32
    l1, l4, l16 = l1_ref[0], l4_ref[0], l16_ref[0]
    mx = jnp.maximum(jnp.maximum(l1, l4), l16)
    e1, e4, e16 = jnp.exp2(l1 - mx), jnp.exp2(l4 - mx), jnp.exp2(l16 - mx)
    den = e1 + e4 + e16
    ob = (e1 * o1_ref[0].astype(f32) + e4 * o4_ref[0].astype(f32)
          + e16 * o16_ref[0].astype(f32)) * (1.0 / den)
    ya = oa_ref[0].astype(f32) * ga_ref[0].astype(f32)
    yb = ob * gb_ref[0].astype(f32)
    y = jnp.concatenate([ya.astype(jnp.bfloat16), yb.astype(jnp.bfloat16), yc_ref[0]], axis=1)
    z = jnp.dot(y, w_ref[...], preferred_element_type=f32)
    out_ref[0] = x_ref[0] + z * _rms_scale(z) * g_ref[...]


def _out_proj(x, oa, o1, o4, o16, l1, l4, l16, ga, gb, yc, w, g):
    b, s, d = x.shape
    tm = PROJ_ROWS
    row = lambda width: pl.BlockSpec((1, tm, width), lambda bi, i: (bi, i, 0))
    return pl.pallas_call(
        _out_proj_kernel,
        grid=(b, s // tm),
        in_specs=[row(d), row(A_W)] + [row(B_W)] * 6 + [row(A_W), row(B_W), row(C_W),
                  pl.BlockSpec((d, d), lambda bi, i: (0, 0)),
                  pl.BlockSpec((1, d), lambda bi, i: (0, 0))],
        out_specs=row(d),
        out_shape=jax.ShapeDtypeStruct((b, s, d), x.dtype),
        compiler_params=pltpu.CompilerParams(
            dimension_semantics=("parallel", "parallel"), vmem_limit_bytes=VMEM_LIMIT),
        name="out_proj",
    )(x, oa, o1, o4, o16, l1, l4, l16, ga, gb, yc, w, g)


def _in_perm():
    ref_off = {}
    off = 0
    for name, wd in (("qa", A_W), ("ka", A_KV_W), ("va", A_KV_W), ("ga", A_W), ("qb", B_W),
                     ("kb", B_W), ("vb", B_W), ("gb", B_W), ("qc", C_W), ("gc", C_W)):
        ref_off[name] = off
        off += wd
    head_perm = np.concatenate([np.arange(h * HEAD_DIM, (h + 1) * HEAD_DIM) for h in A_HEAD_ORDER])
    cols = []
    for name, (c0, c1) in _COLS.items():
        idx = np.arange(c1 - c0)
        if name in ("qa", "ga"):
            idx = head_perm
        cols.append(ref_off[name] + idx)
    return np.concatenate(cols), head_perm


def _rope_tables(seq):
    inv_freq = ROPE_THETA ** (-jnp.arange(0, ROT_DIM, 2, dtype=jnp.float32) / ROT_DIM)
    ang = jnp.arange(seq, dtype=jnp.float32)[:, None] * inv_freq[None, :]
    cos, sin = jnp.cos(ang), jnp.sin(ang)
    pad = HEAD_DIM - ROT_DIM
    cos_h = jnp.concatenate([cos, cos, jnp.ones((seq, pad), jnp.float32)], axis=1)
    sin_h = jnp.concatenate([-sin, sin, jnp.zeros((seq, pad), jnp.float32)], axis=1)
    reps = LANES // HEAD_DIM
    return jnp.tile(cos_h, (1, reps)), jnp.tile(sin_h, (1, reps))


def kernel(x, mem, pre_norm, w_in, sink_a, mem_norm, w_mem_kv, w_out, post_norm):
    b, s, d = x.shape
    depth = pre_norm.shape[0]
    in_perm, head_perm = _in_perm()
    out_perm = np.concatenate([head_perm, np.arange(A_W, d)])
    cos_t, sin_t = _rope_tables(s)
    h = x
    for l in range(depth):
        w_in_k = w_in[l][:, in_perm].astype(jnp.bfloat16)
        w_out_k = w_out[l][out_perm, :].astype(jnp.bfloat16)
        mk, mv = _mem_kv(mem, mem_norm[l][None], w_mem_kv[l].astype(jnp.bfloat16))
        qa, ka, va, qb, kb, vb, ga, gb, yc = _in_proj(
            h, pre_norm[l][None], w_in_k, cos_t, sin_t, mk, mv)
        (oa,) = _band_attn(qa, ka, va, dil=1, max_dist=A_WINDOW - 1, sink=sink_a[l])
        ob = [_band_attn(qb, kb, vb, dil=dil, max_dist=win // dil, want_lse=True)
              for win, dil in B_CONFIGS]
        h = _out_proj(h, oa, ob[0][0], ob[1][0], ob[2][0], ob[0][1], ob[1][1], ob[2][1],
                      ga, gb, yc, w_out_k, post_norm[l][None])
    return h
```

```python
import functools
import math

import jax
import jax.numpy as jnp
import numpy as np
from jax import lax
from jax.experimental import pallas as pl
from jax.experimental.pallas import tpu as pltpu

HEAD_DIM = 64
ROT_DIM = HEAD_DIM // 4
ROPE_THETA = 500000.0
BLOCK = 128
C_HEADS = 4
A_Q_HEADS = 6
A_KV_HEADS = 2
A_WINDOW = 128
B_HEADS = 6
B_CONFIGS = ((128, 1), (512, 4), (2048, 16))
RMS_EPS = 1e-6

LANES = 128
A_W = A_Q_HEADS * HEAD_DIM
A_KV_W = A_KV_HEADS * HEAD_DIM
B_W = B_HEADS * HEAD_DIM
C_W = C_HEADS * HEAD_DIM

A_HEAD_ORDER = (0, 3, 1, 4, 2, 5)

PROJ_ROWS = 512
ATTN_ROWS = 512
VMEM_LIMIT = 48 * 1024 * 1024

LOG2E = math.log2(math.e)
LN2 = math.log(2.0)
Q_SCALE = HEAD_DIM ** -0.5 * LOG2E


def _rms_scale(xf):
    return lax.rsqrt(jnp.mean(xf * xf, axis=-1, keepdims=True) + RMS_EPS)


def _silu(g):
    return g * (1.0 / (1.0 + jnp.exp(-g)))


def _half_select(left):
    lane = lax.broadcasted_iota(jnp.int32, (1, LANES), 1)
    return (lane < HEAD_DIM) if left else (lane >= HEAD_DIM)


def _pair_attention(q2, kg, v_ones, bias, sink2):
    rows = q2.shape[0]
    left = _half_select(True)
    zero = jnp.zeros_like(q2)
    qs = jnp.concatenate([jnp.where(left, q2, zero), jnp.where(left, zero, q2)], axis=0)
    s = lax.dot_general(qs, kg, (((1,), (1,)), ((), ())), preferred_element_type=jnp.float32)
    if bias is not None:
        s = s + bias
    m = jnp.max(s, axis=-1, keepdims=True)
    if sink2 is not None:
        m = jnp.maximum(m, sink2)
    e = jnp.exp2(s - m).astype(jnp.bfloat16)
    pv = jnp.dot(e, v_ones, preferred_element_type=jnp.float32)
    l = pv[:, LANES:]
    if sink2 is not None:
        l = l + jnp.exp2(sink2 - m)
    o = pv[:, :LANES] * (1.0 / l)
    lse2 = m + jnp.log(l) * LOG2E
    return (jnp.where(left, o[:rows], o[rows:]), jnp.where(left, lse2[:rows], lse2[rows:]))


def _mem_kv_kernel(mem_ref, g_ref, w_ref, mk_ref, mv_ref):
    m = mem_ref[0]
    u = (m * _rms_scale(m) * g_ref[...]).astype(jnp.bfloat16)
    kv = jnp.dot(u, w_ref[...], preferred_element_type=jnp.float32)
    mk_ref[0] = kv[:, :C_W].astype(jnp.bfloat16)
    mv_ref[0] = kv[:, C_W:].astype(jnp.bfloat16)


def _mem_kv(mem, g, w):
    b, n_mem, d = mem.shape
    return pl.pallas_call(
        _mem_kv_kernel,
        grid=(b,),
        in_specs=[
            pl.BlockSpec((1, n_mem, d), lambda i: (i, 0, 0)),
            pl.BlockSpec((1, d), lambda i: (0, 0)),
            pl.BlockSpec((d, 2 * C_W), lambda i: (0, 0)),
        ],
        out_specs=[
            pl.BlockSpec((1, n_mem, C_W), lambda i: (i, 0, 0)),
            pl.BlockSpec((1, n_mem, C_W), lambda i: (i, 0, 0)),
        ],
        out_shape=[jax.ShapeDtypeStruct((b, n_mem, C_W), jnp.bfloat16)] * 2,
        name="mem_kv",
    )(mem, g, w)


_COLS = {}
_off = 0
for _name, _w in (("qa", A_W), ("ka", A_KV_W), ("va", A_KV_W), ("qb", B_W), ("kb", B_W),
                  ("vb", B_W), ("qc", C_W), ("ga", A_W), ("gb", B_W), ("gc", C_W)):
    _COLS[_name] = (_off, _off + _w)
    _off += _w
D_IN = _off


def _rope(t, cos_t, sin_t, first8):
    half = ROT_DIM // 2
    out = []
    for j in range(t.shape[1] // LANES):
        tj = t[:, j * LANES:(j + 1) * LANES]
        partner = jnp.where(first8, pltpu.roll(tj, LANES - half, 1), pltpu.roll(tj, half, 1))
        out.append(tj * cos_t + partner * sin_t)
    return out[0] if len(out) == 1 else jnp.concatenate(out, axis=1)


def _store_dilated(t, refs, stage_ref):
    rows = t.shape[0]
    slabs = t.shape[1] // LANES
    for sl in range(slabs):
        stage_ref[sl] = t[:, sl * LANES:(sl + 1) * LANES]
    for dil, ref in refs.items():
        if dil == 1:
            ref[0, 0] = t.astype(ref.dtype)
            continue
        for r in range(dil):
            for sl in range(slabs):
                ref[0, r, :, sl * LANES:(sl + 1) * LANES] = stage_ref[
                    sl, pl.ds(r, rows // dil, stride=dil), :].astype(ref.dtype)


def _in_proj_kernel(x_ref, g_ref, w_ref, cos_ref, sin_ref, mk_ref, mv_ref,
                    qa_ref, ka_ref, va_ref, ga_ref, gb_ref, yc_ref, *rest):
    dils = [d for _, d in B_CONFIGS]
    qb_refs = dict(zip(dils, rest[0:3]))
    kb_refs = dict(zip(dils, rest[3:6]))
    vb_refs = dict(zip(dils, rest[6:9]))
    stage_ref = rest[9]
    xf = x_ref[0]
    u = (xf * _rms_scale(xf) * g_ref[...]).astype(jnp.bfloat16)
    cos_t = cos_ref[...]
    sin_t = sin_ref[...]
    lane = lax.broadcasted_iota(jnp.int32, (1, LANES), 1)
    first8 = (lane % HEAD_DIM) < (ROT_DIM // 2)

    def proj(name):
        c0, c1 = _COLS[name]
        return jnp.dot(u, w_ref[:, c0:c1], preferred_element_type=jnp.float32)

    qa_ref[0] = (_rope(proj("qa"), cos_t, sin_t, first8) * Q_SCALE).astype(jnp.bfloat16)
    ka_ref[0] = _rope(proj("ka"), cos_t, sin_t, first8).astype(jnp.bfloat16)
    va_ref[0] = proj("va").astype(jnp.bfloat16)
    _store_dilated(_rope(proj("qb"), cos_t, sin_t, first8) * Q_SCALE, qb_refs, stage_ref)
    _store_dilated(_rope(proj("kb"), cos_t, sin_t, first8), kb_refs, stage_ref)
    _store_dilated(proj("vb"), vb_refs, stage_ref)
    ga_ref[0] = _silu(proj("ga")).astype(jnp.bfloat16)
    gb_ref[0] = _silu(proj("gb")).astype(jnp.bfloat16)

    qc = (proj("qc") * Q_SCALE).astype(jnp.bfloat16)
    gc = _silu(proj("gc"))
    ones = jnp.ones((mv_ref.shape[1], LANES), jnp.bfloat16)
    oc = []
    for p in range(C_W // LANES):
        cols = slice(p * LANES, (p + 1) * LANES)
        v_ones = jnp.concatenate([mv_ref[0, :, cols], ones], axis=1)
        o, _ = _pair_attention(qc[:, cols], mk_ref[0, :, cols], v_ones, None, None)
        oc.append(o)
    yc_ref[0] = (jnp.concatenate(oc, axis=1) * gc).astype(jnp.bfloat16)


def _in_proj(x, g, w, cos_t, sin_t, mk, mv):
    b, s, d = x.shape
    tm = PROJ_ROWS
    n_mem = mk.shape[1]
    row = lambda width: pl.BlockSpec((1, tm, width), lambda i, bi: (bi, i, 0))
    out_widths = (A_W, A_KV_W, A_KV_W, A_W, B_W, C_W)
    out_specs = [row(wd) for wd in out_widths]
    out_shape = [jax.ShapeDtypeStruct((b, s, wd), jnp.bfloat16) for wd in out_widths]
    for _ in range(3):
        for _, dil in B_CONFIGS:
            out_specs.append(pl.BlockSpec((1, dil, tm // dil, B_W), lambda i, bi: (bi, 0, i, 0)))
            out_shape.append(jax.ShapeDtypeStruct((b, dil, s // dil, B_W), jnp.bfloat16))
    return pl.pallas_call(
        _in_proj_kernel,
        grid=(s // tm, b),
        in_specs=[
            row(d),
            pl.BlockSpec((1, d), lambda i, bi: (0, 0)),
            pl.BlockSpec((d, D_IN), lambda i, bi: (0, 0)),
            pl.BlockSpec((tm, LANES), lambda i, bi: (i, 0)),
            pl.BlockSpec((tm, LANES), lambda i, bi: (i, 0)),
            pl.BlockSpec((1, n_mem, C_W), lambda i, bi: (bi, 0, 0)),
            pl.BlockSpec((1, n_mem, C_W), lambda i, bi: (bi, 0, 0)),
        ],
        out_specs=out_specs,
        out_shape=out_shape,
        scratch_shapes=[pltpu.VMEM((B_W // LANES, tm, LANES), jnp.float32)],
        compiler_params=pltpu.CompilerParams(
            dimension_semantics=("parallel", "parallel"), vmem_limit_bytes=VMEM_LIMIT),
        name="in_proj",
    )(x, g, w, cos_t, sin_t, mk, mv)


def _band_attn_kernel(*refs, max_dist, kv_groups, has_sink, want_lse):
    refs = list(refs)
    sink_ref = refs.pop(0) if has_sink else None
    q_ref, kp_ref, ko_ref, vp_ref, vo_ref, o_ref = refs[:6]
    lse_ref = refs[6] if want_lse else None
    rows = q_ref.shape[1]

    qi = lax.broadcasted_iota(jnp.int32, (BLOCK, 2 * BLOCK), 0)
    kc = lax.broadcasted_iota(jnp.int32, (BLOCK, 2 * BLOCK), 1)
    dist = qi + BLOCK - kc
    band = (dist >= 0) & (dist <= max_dist)
    first_key = jnp.where(pl.program_id(2) == 0, BLOCK, 0)
    neg = jnp.float32(-jnp.inf)
    bias = jnp.where(band, 0.0, neg)
    bias_first = jnp.where(band & (kc >= first_key), 0.0, neg)
    bias = jnp.concatenate([bias, bias], axis=0)
    bias_first = jnp.concatenate([bias_first, bias_first], axis=0)
    ones = jnp.ones((2 * BLOCK, LANES), jnp.bfloat16)

    for i in range(rows // BLOCK):
        r = slice(i * BLOCK, (i + 1) * BLOCK)
        q = q_ref[0, r, :]
        if i == 0:
            k = jnp.concatenate([kp_ref[0], ko_ref[0, r, :]], axis=0)
            v = jnp.concatenate([vp_ref[0], vo_ref[0, r, :]], axis=0)
        else:
            k = ko_ref[0, (i - 1) * BLOCK:(i + 1) * BLOCK, :]
            v = vo_ref[0, (i - 1) * BLOCK:(i + 1) * BLOCK, :]
        for p in range(q.shape[1] // LANES):
            g = p if kv_groups > 1 else 0
            cols = slice(p * LANES, (p + 1) * LANES)
            kcols = slice(g * LANES, (g + 1) * LANES)
            sink2 = None
            if has_sink:
                sink2 = LOG2E * jnp.concatenate(
                    [jnp.full((BLOCK, 1), sink_ref[p], jnp.float32),
                     jnp.full((BLOCK, 1), sink_ref[p + A_Q_HEADS // 2], jnp.float32)], axis=0)
            v_ones = jnp.concatenate([v[:, kcols], ones], axis=1)
            o, lse2 = _pair_attention(q[:, cols], k[:, kcols], v_ones,
                                      bias_first if i == 0 else bias, sink2)
            o_ref[0, r, cols] = o.astype(o_ref.dtype)
            if want_lse:
                lse_ref[0, r, cols] = lse2


def _band_attn(q, k, v, *, dil, max_dist, sink=None, want_lse=False):
    b, _, seq, wq = q.shape
    wk = k.shape[3]
    rows = min(ATTN_ROWS, seq)
    sub = rows // BLOCK
    own = lambda width, n: pl.BlockSpec((1, None, n, width), lambda bi, r, j: (bi, r, j, 0))
    prev = lambda width: pl.BlockSpec(
        (1, None, BLOCK, width), lambda bi, r, j: (bi, r, jnp.maximum(j * sub - 1, 0), 0))
    in_specs = [own(wq, rows), prev(wk), own(wk, rows), prev(wk), own(wk, rows)]
    args = [q, k, k, v, v]
    if sink is not None:
        in_specs.insert(0, pl.BlockSpec(memory_space=pltpu.SMEM))
        args.insert(0, sink)
    out_specs = [own(wq, rows)]
    out_shape = [jax.ShapeDtypeStruct(q.shape, jnp.bfloat16)]
    if want_lse:
        out_specs.append(own(wq, rows))
        out_shape.append(jax.ShapeDtypeStruct(q.shape, jnp.float32))
    return pl.pallas_call(
        functools.partial(_band_attn_kernel, max_dist=max_dist, kv_groups=wk // LANES,
                          has_sink=sink is not None, want_lse=want_lse),
        grid=(b, q.shape[1], seq // rows),
        in_specs=in_specs,
        out_specs=out_specs,
        out_shape=out_shape,
        compiler_params=pltpu.CompilerParams(
            dimension_semantics=("parallel", "parallel", "parallel"),
            vmem_limit_bytes=VMEM_LIMIT),
        name=f"band_attn_d{dil}",
    )(*args)


def _natural_rows(ref, stage_ref):
    dil, n = ref.shape[1], ref.shape[2]
    if dil == 1:
        return ref[0, 0].astype(jnp.float32)
    slabs = ref.shape[3] // LANES
    for r in range(dil):
        t = ref[0, r].astype(jnp.float32)
        for sl in range(slabs):
            stage_ref[sl, pl.ds(r, n, stride=dil), :] = t[:, sl * LANES:(sl + 1) * LANES]
    return jnp.concatenate([stage_ref[sl] for sl in range(slabs)], axis=1)


def _out_proj_kernel(x_ref, oa_ref, o1_ref, o4_ref, o16_ref, l1_ref, l4_ref, l16_ref,
                     ga_ref, gb_ref, yc_ref, w_ref, g_ref, out_ref, *stage):
    f32 = jnp.float32
    l1 = _natural_rows(l1_ref, None)
    l4 = _natural_rows(l4_ref, stage[0])
    l16 = _natural_rows(l16_ref, stage[1])
    mx = jnp.maximum(jnp.maximum(l1, l4), l16)
    e1, e4, e16 = jnp.exp2(l1 - mx), jnp.exp2(l4 - mx), jnp.exp2(l16 - mx)
    den = e1 + e4 + e16
    ob = (e1 * _natural_rows(o1_ref, None) + e4 * _natural_rows(o4_ref, stage[2])
          + e16 * _natural_rows(o16_ref, stage[3])) * (1.0 / den)
    ya = oa_ref[0].astype(f32) * ga_ref[0].astype(f32)
    yb = ob * gb_ref[0].astype(f32)
    y = jnp.concatenate([ya.astype(jnp.bfloat16), yb.astype(jnp.bfloat16), yc_ref[0]], axis=1)
    z = jnp.dot(y, w_ref[...], preferred_element_type=f32)
    out_ref[0] = x_ref[0] + z * _rms_scale(z) * g_ref[...]


def _out_proj(x, oa, o1, o4, o16, l1, l4, l16, ga, gb, yc, w, g):
    b, s, d = x.shape
    tm = PROJ_ROWS
    row = lambda width: pl.BlockSpec((1, tm, width), lambda bi, i: (bi, i, 0))
    dilated = [pl.BlockSpec((1, dil, tm // dil, B_W), lambda bi, i: (bi, 0, i, 0))
               for _, dil in B_CONFIGS]
    return pl.pallas_call(
        _out_proj_kernel,
        grid=(b, s // tm),
        in_specs=[row(d), row(A_W)] + dilated + dilated + [row(A_W), row(B_W), row(C_W),
                  pl.BlockSpec((d, d), lambda bi, i: (0, 0)),
                  pl.BlockSpec((1, d), lambda bi, i: (0, 0))],
        out_specs=row(d),
        out_shape=jax.ShapeDtypeStruct((b, s, d), x.dtype),
        scratch_shapes=[pltpu.VMEM((B_W // LANES, tm, LANES), jnp.float32)] * 4,
        compiler_params=pltpu.CompilerParams(
            dimension_semantics=("parallel", "parallel"), vmem_limit_bytes=VMEM_LIMIT),
        name="out_proj",
    )(x, oa, o1, o4, o16, l1, l4, l16, ga, gb, yc, w, g)


def _in_perm():
    ref_off = {}
    off = 0
    for name, wd in (("qa", A_W), ("ka", A_KV_W), ("va", A_KV_W), ("ga", A_W), ("qb", B_W),
                     ("kb", B_W), ("vb", B_W), ("gb", B_W), ("qc", C_W), ("gc", C_W)):
        ref_off[name] = off
        off += wd
    head_perm = np.concatenate([np.arange(h * HEAD_DIM, (h + 1) * HEAD_DIM) for h in A_HEAD_ORDER])
    cols = []
    for name, (c0, c1) in _COLS.items():
        idx = np.arange(c1 - c0)
        if name in ("qa", "ga"):
            idx = head_perm
        cols.append(ref_off[name] + idx)
    return np.concatenate(cols), head_perm


def _rope_tables(seq):
    inv_freq = ROPE_THETA ** (-jnp.arange(0, ROT_DIM, 2, dtype=jnp.float32) / ROT_DIM)
    ang = jnp.arange(seq, dtype=jnp.float32)[:, None] * inv_freq[None, :]
    cos, sin = jnp.cos(ang), jnp.sin(ang)
    pad = HEAD_DIM - ROT_DIM
    cos_h = jnp.concatenate([cos, cos, jnp.ones((seq, pad), jnp.float32)], axis=1)
    sin_h = jnp.concatenate([-sin, sin, jnp.zeros((seq, pad), jnp.float32)], axis=1)
    reps = LANES // HEAD_DIM
    return jnp.tile(cos_h, (1, reps)), jnp.tile(sin_h, (1, reps))


def kernel(x, mem, pre_norm, w_in, sink_a, mem_norm, w_mem_kv, w_out, post_norm):
    b, s, d = x.shape
    depth = pre_norm.shape[0]
    in_perm, head_perm = _in_perm()
    out_perm = np.concatenate([head_perm, np.arange(A_W, d)])
    cos_t, sin_t = _rope_tables(s)
    h = x
    for l in range(depth):
        w_in_k = w_in[l][:, in_perm].astype(jnp.bfloat16)
        w_out_k = w_out[l][out_perm, :].astype(jnp.bfloat16)
        mk, mv = _mem_kv(mem, mem_norm[l][None], w_mem_kv[l].astype(jnp.bfloat16))
        qa, ka, va, ga, gb, yc, *qkv_b = _in_proj(
            h, pre_norm[l][None], w_in_k, cos_t, sin_t, mk, mv)
        (oa,) = _band_attn(qa[:, None], ka[:, None], va[:, None], dil=1,
                           max_dist=A_WINDOW - 1, sink=sink_a[l])
        n_cfg = len(B_CONFIGS)
        ob = [_band_attn(qkv_b[c], qkv_b[n_cfg + c], qkv_b[2 * n_cfg + c], dil=dil,
                         max_dist=win // dil, want_lse=True)
              for c, (win, dil) in enumerate(B_CONFIGS)]
        h = _out_proj(h, oa[:, 0], ob[0][0], ob[1][0], ob[2][0], ob[0][1], ob[1][1], ob[2][1],
                      ga, gb, yc, w_out_k, post_norm[l][None])
    return h
```

```python
import functools
import math

import jax
import jax.numpy as jnp
from jax import lax
from jax.experimental import pallas as pl
from jax.experimental.pallas import tpu as pltpu

HEAD_DIM = 64
ROT_DIM = HEAD_DIM // 4
ROPE_THETA = 500000.0
BLOCK = 128
C_HEADS = 4
A_Q_HEADS = 6
A_KV_HEADS = 2
A_GROUP = A_Q_HEADS // A_KV_HEADS
A_WINDOW = 128
B_HEADS = 6
B_CONFIGS = ((128, 1), (512, 4), (2048, 16))
RMS_EPS = 1e-6

LANES = 128
A_W = A_Q_HEADS * HEAD_DIM
A_KV_W = A_KV_HEADS * HEAD_DIM
B_W = B_HEADS * HEAD_DIM
C_W = C_HEADS * HEAD_DIM

PROJ_ROWS = 512
ATTN_ROWS = 512
VMEM_LIMIT = 48 * 1024 * 1024

LOG2E = math.log2(math.e)
Q_SCALE = HEAD_DIM ** -0.5 * LOG2E


def _rms_scale(xf):
    return lax.rsqrt(jnp.mean(xf * xf, axis=-1, keepdims=True) + RMS_EPS)


def _silu(g):
    return g * (1.0 / (1.0 + jnp.exp(-g)))


def _left_half():
    return lax.broadcasted_iota(jnp.int32, (1, LANES), 1) < HEAD_DIM


def _pair_attention(q2, kg, v_ones, bias):
    rows = q2.shape[0]
    left = _left_half()
    zero = jnp.zeros_like(q2)
    qs = jnp.concatenate([jnp.where(left, q2, zero), jnp.where(left, zero, q2)], axis=0)
    s = lax.dot_general(qs, kg, (((1,), (1,)), ((), ())), preferred_element_type=jnp.float32)
    if bias is not None:
        s = s + bias
    m = jnp.max(s, axis=-1, keepdims=True)
    e = jnp.exp2(s - m).astype(jnp.bfloat16)
    pv = jnp.dot(e, v_ones, preferred_element_type=jnp.float32)
    l = pv[:, LANES:]
    o = pv[:, :LANES] * (1.0 / l)
    lse2 = m + jnp.log(l) * LOG2E
    return (jnp.where(left, o[:rows], o[rows:]), jnp.where(left, lse2[:rows], lse2[rows:]))


def _mem_kv_kernel(mem_ref, g_ref, w_ref, mk_ref, mv_ref):
    m = mem_ref[0]
    u = (m * _rms_scale(m) * g_ref[...]).astype(jnp.bfloat16)
    kv = jnp.dot(u, w_ref[...], preferred_element_type=jnp.float32)
    mk_ref[0] = kv[:, :C_W].astype(jnp.bfloat16)
    mv_ref[0] = kv[:, C_W:].astype(jnp.bfloat16)


def _mem_kv(mem, g, w):
    b, n_mem, d = mem.shape
    return pl.pallas_call(
        _mem_kv_kernel,
        grid=(b,),
        in_specs=[
            pl.BlockSpec((1, n_mem, d), lambda i: (i, 0, 0)),
            pl.BlockSpec((1, d), lambda i: (0, 0)),
            pl.BlockSpec((d, 2 * C_W), lambda i: (0, 0)),
        ],
        out_specs=[
            pl.BlockSpec((1, n_mem, C_W), lambda i: (i, 0, 0)),
            pl.BlockSpec((1, n_mem, C_W), lambda i: (i, 0, 0)),
        ],
        out_shape=[jax.ShapeDtypeStruct((b, n_mem, C_W), jnp.bfloat16)] * 2,
        name="mem_kv",
    )(mem, g, w)


_IN_WIDTHS = (("qa", A_W), ("ka", A_KV_W), ("va", A_KV_W), ("ga", A_W),
              ("qb", B_W), ("kb", B_W), ("vb", B_W), ("gb", B_W), ("qc", C_W), ("gc", C_W))
D_IN = sum(w for _, w in _IN_WIDTHS)
_DOT_GROUPS = (("qa", "ka"), ("va", "ga"), ("qb", "kb"), ("vb", "gb"), ("qc", "gc"))


def _rope(t, cos_t, sin_t, first8):
    half = ROT_DIM // 2
    out = []
    for j in range(t.shape[1] // LANES):
        tj = t[:, j * LANES:(j + 1) * LANES]
        partner = jnp.where(first8, pltpu.roll(tj, LANES - half, 1), pltpu.roll(tj, half, 1))
        out.append(tj * cos_t + partner * sin_t)
    return out[0] if len(out) == 1 else jnp.concatenate(out, axis=1)


def _expand_kv(t):
    left = _left_half()
    swapped = pltpu.roll(t, HEAD_DIM, 1)
    groups = []
    for p in range(A_Q_HEADS // 2):
        kv_l, kv_r = (2 * p) // A_GROUP, (2 * p + 1) // A_GROUP
        if (kv_l, kv_r) == (0, 1):
            groups.append(t)
        elif kv_l == kv_r == 0:
            groups.append(jnp.where(left, t, swapped))
        else:
            groups.append(jnp.where(left, swapped, t))
    return jnp.concatenate(groups, axis=1)


def _store_dilated(t, refs, stage_ref, stage4_ref):
    rows = t.shape[0]
    slabs = t.shape[1] // LANES
    refs[1][0, 0] = t.astype(refs[1].dtype)
    for sl in range(slabs):
        stage_ref[sl] = t[:, sl * LANES:(sl + 1) * LANES]
    n4, n16 = rows // 4, rows // 16
    for sl in range(slabs):
        cols = slice(sl * LANES, (sl + 1) * LANES)
        for r in range(4):
            c4 = stage_ref[sl, pl.ds(r, n4, stride=4), :]
            refs[4][0, r, :, cols] = c4.astype(refs[4].dtype)
            stage4_ref[sl, r * n4:(r + 1) * n4, :] = c4
        for r in range(4):
            for q in range(4):
                c16 = stage4_ref[sl, pl.ds(r * n4 + q, n16, stride=4), :]
                refs[16][0, r + 4 * q, :, cols] = c16.astype(refs[16].dtype)


def _in_proj_kernel(x_ref, g_ref, w_ref, cos_ref, sin_ref, mk_ref, mv_ref,
                    qa_ref, ka_ref, va_ref, ga_ref, gb_ref, yc_ref, *rest):
    dils = [d for _, d in B_CONFIGS]
    qb_refs = dict(zip(dils, rest[0:3]))
    kb_refs = dict(zip(dils, rest[3:6]))
    vb_refs = dict(zip(dils, rest[6:9]))
    stage_ref, stage4_ref = rest[9:11]
    bf16 = jnp.bfloat16
    xf = x_ref[0]
    u = (xf * _rms_scale(xf) * g_ref[...]).astype(bf16)
    cos_t = cos_ref[...]
    sin_t = sin_ref[...]
    lane = lax.broadcasted_iota(jnp.int32, (1, LANES), 1)
    first8 = (lane % HEAD_DIM) < (ROT_DIM // 2)
    rope = lambda t: _rope(t, cos_t, sin_t, first8)

    offsets = {}
    off = 0
    for name, wd in _IN_WIDTHS:
        offsets[name] = (off, wd)
        off += wd

    def proj(names):
        c0 = offsets[names[0]][0]
        c1 = offsets[names[-1]][0] + offsets[names[-1]][1]
        full = jnp.dot(u, w_ref[:, c0:c1], preferred_element_type=jnp.float32)
        return [full[:, offsets[n][0] - c0:offsets[n][0] - c0 + offsets[n][1]] for n in names]

    qa, ka = proj(_DOT_GROUPS[0])
    qa_ref[0] = (rope(qa) * Q_SCALE).astype(bf16)
    ka_ref[0] = _expand_kv(rope(ka)).astype(bf16)
    va, ga = proj(_DOT_GROUPS[1])
    va_ref[0] = _expand_kv(va).astype(bf16)
    ga_ref[0] = _silu(ga).astype(bf16)
    qb, kb = proj(_DOT_GROUPS[2])
    _store_dilated(rope(qb) * Q_SCALE, qb_refs, stage_ref, stage4_ref)
    _store_dilated(rope(kb), kb_refs, stage_ref, stage4_ref)
    vb, gb = proj(_DOT_GROUPS[3])
    _store_dilated(vb, vb_refs, stage_ref, stage4_ref)
    gb_ref[0] = _silu(gb).astype(bf16)

    qc, gc = proj(_DOT_GROUPS[4])
    qc = (qc * Q_SCALE).astype(bf16)
    ones = jnp.ones((mv_ref.shape[1], LANES), bf16)
    oc = []
    for p in range(C_W // LANES):
        cols = slice(p * LANES, (p + 1) * LANES)
        v_ones = jnp.concatenate([mv_ref[0, :, cols], ones], axis=1)
        o, _ = _pair_attention(qc[:, cols], mk_ref[0, :, cols], v_ones, None)
        oc.append(o)
    yc_ref[0] = (jnp.concatenate(oc, axis=1) * _silu(gc)).astype(bf16)


def _in_proj(x, g, w, cos_t, sin_t, mk, mv):
    b, s, d = x.shape
    tm = PROJ_ROWS
    n_mem = mk.shape[1]
    row = lambda width: pl.BlockSpec((1, tm, width), lambda i, bi: (bi, i, 0))
    out_widths = (A_W, A_W, A_W, A_W, B_W, C_W)
    out_specs = [row(wd) for wd in out_widths]
    out_shape = [jax.ShapeDtypeStruct((b, s, wd), jnp.bfloat16) for wd in out_widths]
    for _ in range(3):
        for _, dil in B_CONFIGS:
            out_specs.append(pl.BlockSpec((1, dil, tm // dil, B_W), lambda i, bi: (bi, 0, i, 0)))
            out_shape.append(jax.ShapeDtypeStruct((b, dil, s // dil, B_W), jnp.bfloat16))
    stage = pltpu.VMEM((B_W // LANES, tm, LANES), jnp.float32)
    return pl.pallas_call(
        _in_proj_kernel,
        grid=(s // tm, b),
        in_specs=[
            row(d),
            pl.BlockSpec((1, d), lambda i, bi: (0, 0)),
            pl.BlockSpec((d, D_IN), lambda i, bi: (0, 0)),
            pl.BlockSpec((tm, LANES), lambda i, bi: (i, 0)),
            pl.BlockSpec((tm, LANES), lambda i, bi: (i, 0)),
            pl.BlockSpec((1, n_mem, C_W), lambda i, bi: (bi, 0, 0)),
            pl.BlockSpec((1, n_mem, C_W), lambda i, bi: (bi, 0, 0)),
        ],
        out_specs=out_specs,
        out_shape=out_shape,
        scratch_shapes=[stage, stage],
        compiler_params=pltpu.CompilerParams(
            dimension_semantics=("parallel", "parallel"), vmem_limit_bytes=VMEM_LIMIT),
        name="in_proj",
    )(x, g, w, cos_t, sin_t, mk, mv)


def _band_attn_kernel(*refs, max_dist, has_sink, want_lse):
    refs = list(refs)
    sink_ref = refs.pop(0) if has_sink else None
    q_ref, kp_ref, ko_ref, vp_ref, vo_ref, o_ref = refs[:6]
    lse_ref = refs[6] if want_lse else None
    rows = q_ref.shape[1]
    groups = q_ref.shape[2] // LANES

    qi = lax.broadcasted_iota(jnp.int32, (BLOCK, 2 * BLOCK), 0)
    kc = lax.broadcasted_iota(jnp.int32, (BLOCK, 2 * BLOCK), 1)
    dist = qi + BLOCK - kc
    band = (dist >= 0) & (dist <= max_dist)
    first_key = jnp.where(pl.program_id(2) == 0, BLOCK, 0)
    neg = jnp.float32(-jnp.inf)
    bias_any = jnp.where(band, 0.0, neg)
    bias_first = jnp.where(band & (kc >= first_key), 0.0, neg)
    ones = jnp.ones((2 * BLOCK, LANES), jnp.bfloat16)

    def pair_bias(base, p):
        if not has_sink:
            return jnp.concatenate([base, base], axis=0)
        assert max_dist < BLOCK
        halves = [jnp.where(kc == 0, sink_ref[2 * p + h] * LOG2E, base) for h in range(2)]
        return jnp.concatenate(halves, axis=0)

    bias_rest = [pair_bias(bias_any, p) for p in range(groups)]
    bias_0 = [pair_bias(bias_first, p) for p in range(groups)]
    row0 = lax.broadcasted_iota(jnp.int32, (BLOCK, 1), 0) == 0

    for i in range(rows // BLOCK):
        r = slice(i * BLOCK, (i + 1) * BLOCK)
        q = q_ref[0, r, :]
        if i == 0:
            k_prev, v_prev = kp_ref[0], vp_ref[0]
        else:
            k_prev, v_prev = ko_ref[0, (i - 1) * BLOCK:i * BLOCK, :], vo_ref[0, (i - 1) * BLOCK:i * BLOCK, :]
        if has_sink:
            k_prev = jnp.where(row0, jnp.zeros_like(k_prev), k_prev)
            v_prev = jnp.where(row0, jnp.zeros_like(v_prev), v_prev)
        k = jnp.concatenate([k_prev, ko_ref[0, r, :]], axis=0)
        v = jnp.concatenate([v_prev, vo_ref[0, r, :]], axis=0)
        for p in range(groups):
            cols = slice(p * LANES, (p + 1) * LANES)
            v_ones = jnp.concatenate([v[:, cols], ones], axis=1)
            o, lse2 = _pair_attention(q[:, cols], k[:, cols], v_ones,
                                      bias_0[p] if i == 0 else bias_rest[p])
            o_ref[0, r, cols] = o.astype(o_ref.dtype)
            if want_lse:
                lse_ref[0, r, cols] = lse2


def _band_attn(q, k, v, *, dil, max_dist, sink=None, want_lse=False):
    b, _, seq, wq = q.shape
    rows = min(ATTN_ROWS, seq)
    sub = rows // BLOCK
    own = lambda n: pl.BlockSpec((1, None, n, wq), lambda bi, r, j: (bi, r, j, 0))
    prev = pl.BlockSpec(
        (1, None, BLOCK, wq), lambda bi, r, j: (bi, r, jnp.maximum(j * sub - 1, 0), 0))
    in_specs = [own(rows), prev, own(rows), prev, own(rows)]
    args = [q, k, k, v, v]
    if sink is not None:
        in_specs.insert(0, pl.BlockSpec(memory_space=pltpu.SMEM))
        args.insert(0, sink)
    out_specs = [own(rows)]
    out_shape = [jax.ShapeDtypeStruct(q.shape, jnp.bfloat16)]
    if want_lse:
        out_specs.append(own(rows))
        out_shape.append(jax.ShapeDtypeStruct(q.shape, jnp.float32))
    return pl.pallas_call(
        functools.partial(_band_attn_kernel, max_dist=max_dist,
                          has_sink=sink is not None, want_lse=want_lse),
        grid=(b, q.shape[1], seq // rows),
        in_specs=in_specs,
        out_specs=out_specs,
        out_shape=out_shape,
        compiler_params=pltpu.CompilerParams(
            dimension_semantics=("parallel", "parallel", "parallel"),
            vmem_limit_bytes=VMEM_LIMIT),
        name=f"band_attn_{'a' if sink is not None else 'b'}{dil}",
    )(*args)


def _natural_rows(ref, stage_ref):
    dil, n = ref.shape[1], ref.shape[2]
    if dil == 1:
        return ref[0, 0].astype(jnp.float32)
    slabs = ref.shape[3] // LANES
    for r in range(dil):
        t = ref[0, r].astype(jnp.float32)
        for sl in range(slabs):
            stage_ref[sl, pl.ds(r, n, stride=dil), :] = t[:, sl * LANES:(sl + 1) * LANES]
    return jnp.concatenate([stage_ref[sl] for sl in range(slabs)], axis=1)


def _out_proj_kernel(x_ref, oa_ref, o1_ref, o4_ref, o16_ref, l1_ref, l4_ref, l16_ref,
                     ga_ref, gb_ref, yc_ref, w_ref, g_ref, out_ref, *stage):
    f32 = jnp.float32
    l1 = _natural_rows(l1_ref, None)
    l4 = _natural_rows(l4_ref, stage[0])
    l16 = _natural_rows(l16_ref, stage[1])
    mx = jnp.maximum(jnp.maximum(l1, l4), l16)
    e1, e4, e16 = jnp.exp2(l1 - mx), jnp.exp2(l4 - mx), jnp.exp2(l16 - mx)
    den = e1 + e4 + e16
    ob = (e1 * _natural_rows(o1_ref, None) + e4 * _natural_rows(o4_ref, stage[2])
          + e16 * _natural_rows(o16_ref, stage[3])) * (1.0 / den)
    ya = oa_ref[0].astype(f32) * ga_ref[0].astype(f32)
    yb = ob * gb_ref[0].astype(f32)
    y = jnp.concatenate([ya.astype(jnp.bfloat16), yb.astype(jnp.bfloat16), yc_ref[0]], axis=1)
    z = jnp.dot(y, w_ref[...], preferred_element_type=f32)
    out_ref[0] = x_ref[0] + z * _rms_scale(z) * g_ref[...]


def _out_proj(x, oa, o1, o4, o16, l1, l4, l16, ga, gb, yc, w, g):
    b, s, d = x.shape
    tm = PROJ_ROWS
    row = lambda width: pl.BlockSpec((1, tm, width), lambda bi, i: (bi, i, 0))
    dilated = [pl.BlockSpec((1, dil, tm // dil, B_W), lambda bi, i: (bi, 0, i, 0))
               for _, dil in B_CONFIGS]
    return pl.pallas_call(
        _out_proj_kernel,
        grid=(b, s // tm),
        in_specs=[row(d), row(A_W)] + dilated + dilated + [row(A_W), row(B_W), row(C_W),
                  pl.BlockSpec((d, d), lambda bi, i: (0, 0)),
                  pl.BlockSpec((1, d), lambda bi, i: (0, 0))],
        out_specs=row(d),
        out_shape=jax.ShapeDtypeStruct((b, s, d), x.dtype),
        scratch_shapes=[pltpu.VMEM((B_W // LANES, tm, LANES), jnp.float32)] * 4,
        compiler_params=pltpu.CompilerParams(
            dimension_semantics=("parallel", "parallel"), vmem_limit_bytes=VMEM_LIMIT),
        name="out_proj",
    )(x, oa, o1, o4, o16, l1, l4, l16, ga, gb, yc, w, g)


def _rope_tables(seq):
    inv_freq = ROPE_THETA ** (-jnp.arange(0, ROT_DIM, 2, dtype=jnp.float32) / ROT_DIM)
    ang = jnp.arange(seq, dtype=jnp.float32)[:, None] * inv_freq[None, :]
    cos, sin = jnp.cos(ang), jnp.sin(ang)
    pad = HEAD_DIM - ROT_DIM
    cos_h = jnp.concatenate([cos, cos, jnp.ones((seq, pad), jnp.float32)], axis=1)
    sin_h = jnp.concatenate([-sin, sin, jnp.zeros((seq, pad), jnp.float32)], axis=1)
    reps = LANES // HEAD_DIM
    return jnp.tile(cos_h, (1, reps)), jnp.tile(sin_h, (1, reps))


def kernel(x, mem, pre_norm, w_in, sink_a, mem_norm, w_mem_kv, w_out, post_norm):
    depth = pre_norm.shape[0]
    cos_t, sin_t = _rope_tables(x.shape[1])
    bf16 = jnp.bfloat16
    h = x
    for l in range(depth):
        mk, mv = _mem_kv(mem, mem_norm[l][None], w_mem_kv[l].astype(bf16))
        qa, ka, va, ga, gb, yc, *qkv_b = _in_proj(
            h, pre_norm[l][None], w_in[l].astype(bf16), cos_t, sin_t, mk, mv)
        (oa,) = _band_attn(qa[:, None], ka[:, None], va[:, None], dil=1,
                           max_dist=A_WINDOW - 1, sink=sink_a[l])
        n_cfg = len(B_CONFIGS)
        ob = [_band_attn(qkv_b[c], qkv_b[n_cfg + c], qkv_b[2 * n_cfg + c], dil=dil,
                         max_dist=win // dil, want_lse=True)
              for c, (win, dil) in enumerate(B_CONFIGS)]
        h = _out_proj(h, oa[:, 0], ob[0][0], ob[1][0], ob[2][0], ob[0][1], ob[1][1], ob[2][1],
                      ga, gb, yc, w_out[l].astype(bf16), post_norm[l][None])
    return h
```

```python
import functools
import math

import jax
import jax.numpy as jnp
from jax import lax
from jax.experimental import pallas as pl
from jax.experimental.pallas import tpu as pltpu

HEAD_DIM = 64
ROT_DIM = HEAD_DIM // 4
ROPE_THETA = 500000.0
BLOCK = 128
C_HEADS = 4
A_Q_HEADS = 6
A_KV_HEADS = 2
A_GROUP = A_Q_HEADS // A_KV_HEADS
A_WINDOW = 128
B_HEADS = 6
B_CONFIGS = ((128, 1), (512, 4), (2048, 16))
RMS_EPS = 1e-6

LANES = 128
A_W = A_Q_HEADS * HEAD_DIM
A_KV_W = A_KV_HEADS * HEAD_DIM
B_W = B_HEADS * HEAD_DIM
C_W = C_HEADS * HEAD_DIM

PROJ_ROWS = 512
ATTN_ROWS = 2048
VMEM_LIMIT = 48 * 1024 * 1024

LOG2E = math.log2(math.e)
Q_SCALE = HEAD_DIM ** -0.5 * LOG2E


def _rms_scale(xf):
    return lax.rsqrt(jnp.mean(xf * xf, axis=-1, keepdims=True) + RMS_EPS)


def _silu(g):
    return g * (1.0 / (1.0 + jnp.exp(-g)))


def _left_half():
    return lax.broadcasted_iota(jnp.int32, (1, LANES), 1) < HEAD_DIM


def _pair_attention(q2, kg, v_ones, bias):
    rows = q2.shape[0]
    left = _left_half()
    zero = jnp.zeros_like(q2)
    qs = jnp.concatenate([jnp.where(left, q2, zero), jnp.where(left, zero, q2)], axis=0)
    s = lax.dot_general(qs, kg, (((1,), (1,)), ((), ())), preferred_element_type=jnp.float32)
    if bias is not None:
        s = s + bias
    m = jnp.max(s, axis=-1, keepdims=True)
    e = jnp.exp2((s - m).astype(jnp.bfloat16))
    pv = jnp.dot(e, v_ones, preferred_element_type=jnp.float32)
    acc = jnp.where(left, pv[:rows, :LANES], pv[rows:, :LANES])
    l = jnp.where(left, pv[:rows, LANES:], pv[rows:, LANES:])
    m_own = jnp.where(left, m[:rows], m[rows:])
    return acc * (1.0 / l), m_own + jnp.log(l) * LOG2E


def _mem_kv_kernel(mem_ref, g_ref, w_ref, mk_ref, mv_ref):
    m = mem_ref[0]
    u = (m * _rms_scale(m) * g_ref[...]).astype(jnp.bfloat16)
    kv = jnp.dot(u, w_ref[...], preferred_element_type=jnp.float32)
    mk_ref[0] = kv[:, :C_W].astype(jnp.bfloat16)
    mv_ref[0] = kv[:, C_W:].astype(jnp.bfloat16)


def _mem_kv(mem, g, w):
    b, n_mem, d = mem.shape
    return pl.pallas_call(
        _mem_kv_kernel,
        grid=(b,),
        in_specs=[
            pl.BlockSpec((1, n_mem, d), lambda i: (i, 0, 0)),
            pl.BlockSpec((1, d), lambda i: (0, 0)),
            pl.BlockSpec((d, 2 * C_W), lambda i: (0, 0)),
        ],
        out_specs=[
            pl.BlockSpec((1, n_mem, C_W), lambda i: (i, 0, 0)),
            pl.BlockSpec((1, n_mem, C_W), lambda i: (i, 0, 0)),
        ],
        out_shape=[jax.ShapeDtypeStruct((b, n_mem, C_W), jnp.bfloat16)] * 2,
        name="mem_kv",
    )(mem, g, w)


_IN_WIDTHS = (("qa", A_W), ("ka", A_KV_W), ("va", A_KV_W), ("ga", A_W),
              ("qb", B_W), ("kb", B_W), ("vb", B_W), ("gb", B_W), ("qc", C_W), ("gc", C_W))
D_IN = sum(w for _, w in _IN_WIDTHS)
_DOT_GROUPS = (("qa", "ka"), ("va", "ga"), ("qb", "kb"), ("vb", "gb"), ("qc", "gc"))


def _rope(t, cos_t, sin_t, first8):
    half = ROT_DIM // 2
    out = []
    for j in range(t.shape[1] // LANES):
        tj = t[:, j * LANES:(j + 1) * LANES]
        partner = jnp.where(first8, pltpu.roll(tj, LANES - half, 1), pltpu.roll(tj, half, 1))
        out.append(tj * cos_t + partner * sin_t)
    return out[0] if len(out) == 1 else jnp.concatenate(out, axis=1)


def _expand_kv(t):
    left = _left_half()
    swapped = pltpu.roll(t, HEAD_DIM, 1)
    groups = []
    for p in range(A_Q_HEADS // 2):
        kv_l, kv_r = (2 * p) // A_GROUP, (2 * p + 1) // A_GROUP
        if (kv_l, kv_r) == (0, 1):
            groups.append(t)
        elif kv_l == kv_r == 0:
            groups.append(jnp.where(left, t, swapped))
        else:
            groups.append(jnp.where(left, swapped, t))
    return jnp.concatenate(groups, axis=1)


def _store_dilated(t, refs, stage_ref, stage4_ref):
    rows = t.shape[0]
    slabs = t.shape[1] // LANES
    refs[1][0, 0] = t.astype(refs[1].dtype)
    for sl in range(slabs):
        stage_ref[sl] = t[:, sl * LANES:(sl + 1) * LANES]
    n4, n16 = rows // 4, rows // 16
    for sl in range(slabs):
        cols = slice(sl * LANES, (sl + 1) * LANES)
        for r in range(4):
            c4 = stage_ref[sl, pl.ds(r, n4, stride=4), :]
            refs[4][0, r, :, cols] = c4.astype(refs[4].dtype)
            stage4_ref[sl, r * n4:(r + 1) * n4, :] = c4
        for r in range(4):
            for q in range(4):
                c16 = stage4_ref[sl, pl.ds(r * n4 + q, n16, stride=4), :]
                refs[16][0, r + 4 * q, :, cols] = c16.astype(refs[16].dtype)


def _in_proj_kernel(x_ref, g_ref, w_ref, cos_ref, sin_ref, mk_ref, mv_ref,
                    qa_ref, ka_ref, va_ref, ga_ref, gb_ref, yc_ref, *rest):
    dils = [d for _, d in B_CONFIGS]
    qb_refs = dict(zip(dils, rest[0:3]))
    kb_refs = dict(zip(dils, rest[3:6]))
    vb_refs = dict(zip(dils, rest[6:9]))
    stage_ref, stage4_ref = rest[9:11]
    bf16 = jnp.bfloat16
    xf = x_ref[0]
    u = (xf * _rms_scale(xf) * g_ref[...]).astype(bf16)
    cos_t = cos_ref[...]
    sin_t = sin_ref[...]
    lane = lax.broadcasted_iota(jnp.int32, (1, LANES), 1)
    first8 = (lane % HEAD_DIM) < (ROT_DIM // 2)
    rope = lambda t: _rope(t, cos_t, sin_t, first8)

    offsets = {}
    off = 0
    for name, wd in _IN_WIDTHS:
        offsets[name] = (off, wd)
        off += wd

    def proj(names):
        c0 = offsets[names[0]][0]
        c1 = offsets[names[-1]][0] + offsets[names[-1]][1]
        full = jnp.dot(u, w_ref[:, c0:c1], preferred_element_type=jnp.float32)
        return [full[:, offsets[n][0] - c0:offsets[n][0] - c0 + offsets[n][1]] for n in names]

    qa, ka = proj(_DOT_GROUPS[0])
    qa_ref[0] = (rope(qa) * Q_SCALE).astype(bf16)
    ka_ref[0] = _expand_kv(rope(ka)).astype(bf16)
    va, ga = proj(_DOT_GROUPS[1])
    va_ref[0] = _expand_kv(va).astype(bf16)
    ga_ref[0] = _silu(ga).astype(bf16)
    qb, kb = proj(_DOT_GROUPS[2])
    _store_dilated(rope(qb) * Q_SCALE, qb_refs, stage_ref, stage4_ref)
    _store_dilated(rope(kb), kb_refs, stage_ref, stage4_ref)
    vb, gb = proj(_DOT_GROUPS[3])
    _store_dilated(vb, vb_refs, stage_ref, stage4_ref)
    gb_ref[0] = _silu(gb).astype(bf16)

    qc, gc = proj(_DOT_GROUPS[4])
    qc = (qc * Q_SCALE).astype(bf16)
    ones = jnp.ones((mv_ref.shape[1], LANES), bf16)
    oc = []
    for p in range(C_W // LANES):
        cols = slice(p * LANES, (p + 1) * LANES)
        v_ones = jnp.concatenate([mv_ref[0, :, cols], ones], axis=1)
        o, _ = _pair_attention(qc[:, cols], mk_ref[0, :, cols], v_ones, None)
        oc.append(o)
    yc_ref[0] = (jnp.concatenate(oc, axis=1) * _silu(gc)).astype(bf16)


def _in_proj(x, g, w, cos_t, sin_t, mk, mv):
    b, s, d = x.shape
    tm = PROJ_ROWS
    n_mem = mk.shape[1]
    row = lambda width: pl.BlockSpec((1, tm, width), lambda i, bi: (bi, i, 0))
    out_widths = (A_W, A_W, A_W, A_W, B_W, C_W)
    out_specs = [row(wd) for wd in out_widths]
    out_shape = [jax.ShapeDtypeStruct((b, s, wd), jnp.bfloat16) for wd in out_widths]
    for _ in range(3):
        for _, dil in B_CONFIGS:
            out_specs.append(pl.BlockSpec((1, dil, tm // dil, B_W), lambda i, bi: (bi, 0, i, 0)))
            out_shape.append(jax.ShapeDtypeStruct((b, dil, s // dil, B_W), jnp.bfloat16))
    stage = pltpu.VMEM((B_W // LANES, tm, LANES), jnp.float32)
    return pl.pallas_call(
        _in_proj_kernel,
        grid=(s // tm, b),
        in_specs=[
            row(d),
            pl.BlockSpec((1, d), lambda i, bi: (0, 0)),
            pl.BlockSpec((d, D_IN), lambda i, bi: (0, 0)),
            pl.BlockSpec((tm, LANES), lambda i, bi: (i, 0)),
            pl.BlockSpec((tm, LANES), lambda i, bi: (i, 0)),
            pl.BlockSpec((1, n_mem, C_W), lambda i, bi: (bi, 0, 0)),
            pl.BlockSpec((1, n_mem, C_W), lambda i, bi: (bi, 0, 0)),
        ],
        out_specs=out_specs,
        out_shape=out_shape,
        scratch_shapes=[stage, stage],
        compiler_params=pltpu.CompilerParams(
            dimension_semantics=("parallel", "parallel"), vmem_limit_bytes=VMEM_LIMIT),
        name="in_proj",
    )(x, g, w, cos_t, sin_t, mk, mv)


def _band_attn_kernel(*refs, max_dist, has_sink, want_lse):
    refs = list(refs)
    sink_ref = refs.pop(0) if has_sink else None
    q_ref, kp_ref, ko_ref, vp_ref, vo_ref, o_ref = refs[:6]
    lse_ref = refs[6] if want_lse else None
    rows = q_ref.shape[2]
    groups = q_ref.shape[3] // LANES

    qi = lax.broadcasted_iota(jnp.int32, (BLOCK, 2 * BLOCK), 0)
    kc = lax.broadcasted_iota(jnp.int32, (BLOCK, 2 * BLOCK), 1)
    dist = qi + BLOCK - kc
    band = (dist >= 0) & (dist <= max_dist)
    first_key = jnp.where(pl.program_id(2) == 0, BLOCK, 0)
    neg = jnp.float32(-jnp.inf)
    bias_any = jnp.where(band, 0.0, neg)
    bias_first = jnp.where(band & (kc >= first_key), 0.0, neg)
    ones = jnp.ones((2 * BLOCK, LANES), jnp.bfloat16)

    def pair_bias(base, p):
        if not has_sink:
            return jnp.concatenate([base, base], axis=0)
        assert max_dist < BLOCK
        halves = [jnp.where(kc == 0, sink_ref[2 * p + h] * LOG2E, base) for h in range(2)]
        return jnp.concatenate(halves, axis=0)

    bias_rest = [pair_bias(bias_any, p) for p in range(groups)]
    bias_0 = [pair_bias(bias_first, p) for p in range(groups)]
    row0 = lax.broadcasted_iota(jnp.int32, (BLOCK, 1), 0) == 0

    for c in range(q_ref.shape[1]):
        for i in range(rows // BLOCK):
            r = slice(i * BLOCK, (i + 1) * BLOCK)
            q = q_ref[0, c, r, :]
            if i == 0:
                k_prev, v_prev = kp_ref[0, c], vp_ref[0, c]
            else:
                before = slice((i - 1) * BLOCK, i * BLOCK)
                k_prev, v_prev = ko_ref[0, c, before, :], vo_ref[0, c, before, :]
            if has_sink:
                k_prev = jnp.where(row0, jnp.zeros_like(k_prev), k_prev)
                v_prev = jnp.where(row0, jnp.zeros_like(v_prev), v_prev)
            k = jnp.concatenate([k_prev, ko_ref[0, c, r, :]], axis=0)
            v = jnp.concatenate([v_prev, vo_ref[0, c, r, :]], axis=0)
            for p in range(groups):
                cols = slice(p * LANES, (p + 1) * LANES)
                v_ones = jnp.concatenate([v[:, cols], ones], axis=1)
                o, lse2 = _pair_attention(q[:, cols], k[:, cols], v_ones,
                                          bias_0[p] if i == 0 else bias_rest[p])
                o_ref[0, c, r, cols] = o.astype(o_ref.dtype)
                if want_lse:
                    lse_ref[0, c, r, cols] = lse2


def _band_attn(q, k, v, *, dil, max_dist, sink=None, want_lse=False):
    b, n_seq, seq, wq = q.shape
    rows = min(ATTN_ROWS, seq)
    sub = rows // BLOCK
    seqs = min(ATTN_ROWS // rows, n_seq)
    own = lambda n: pl.BlockSpec((1, seqs, n, wq), lambda bi, r, j: (bi, r, j, 0))
    prev = pl.BlockSpec(
        (1, seqs, BLOCK, wq), lambda bi, r, j: (bi, r, jnp.maximum(j * sub - 1, 0), 0))
    in_specs = [own(rows), prev, own(rows), prev, own(rows)]
    args = [q, k, k, v, v]
    if sink is not None:
        in_specs.insert(0, pl.BlockSpec(memory_space=pltpu.SMEM))
        args.insert(0, sink)
    out_specs = [own(rows)]
    out_shape = [jax.ShapeDtypeStruct(q.shape, jnp.bfloat16)]
    if want_lse:
        out_specs.append(own(rows))
        out_shape.append(jax.ShapeDtypeStruct(q.shape, jnp.float32))
    return pl.pallas_call(
        functools.partial(_band_attn_kernel, max_dist=max_dist,
                          has_sink=sink is not None, want_lse=want_lse),
        grid=(b, n_seq // seqs, seq // rows),
        in_specs=in_specs,
        out_specs=out_specs,
        out_shape=out_shape,
        compiler_params=pltpu.CompilerParams(
            dimension_semantics=("parallel", "parallel", "parallel"),
            vmem_limit_bytes=VMEM_LIMIT),
        name=f"band_attn_{'a' if sink is not None else 'b'}{dil}",
    )(*args)


def _natural_rows(ref, stage_ref):
    dil, n = ref.shape[1], ref.shape[2]
    if dil == 1:
        return ref[0, 0].astype(jnp.float32)
    slabs = ref.shape[3] // LANES
    for r in range(dil):
        t = ref[0, r].astype(jnp.float32)
        for sl in range(slabs):
            stage_ref[sl, pl.ds(r, n, stride=dil), :] = t[:, sl * LANES:(sl + 1) * LANES]
    return jnp.concatenate([stage_ref[sl] for sl in range(slabs)], axis=1)


def _out_proj_kernel(x_ref, oa_ref, o1_ref, o4_ref, o16_ref, l1_ref, l4_ref, l16_ref,
                     ga_ref, gb_ref, yc_ref, w_ref, g_ref, out_ref, *stage):
    f32 = jnp.float32
    l1 = _natural_rows(l1_ref, None)
    l4 = _natural_rows(l4_ref, stage[0])
    l16 = _natural_rows(l16_ref, stage[1])
    mx = jnp.maximum(jnp.maximum(l1, l4), l16)
    e1, e4, e16 = jnp.exp2(l1 - mx), jnp.exp2(l4 - mx), jnp.exp2(l16 - mx)
    den = e1 + e4 + e16
    ob = (e1 * _natural_rows(o1_ref, None) + e4 * _natural_rows(o4_ref, stage[2])
          + e16 * _natural_rows(o16_ref, stage[3])) * (1.0 / den)
    ya = oa_ref[0].astype(f32) * ga_ref[0].astype(f32)
    yb = ob * gb_ref[0].astype(f32)
    y = jnp.concatenate([ya.astype(jnp.bfloat16), yb.astype(jnp.bfloat16), yc_ref[0]], axis=1)
    z = jnp.dot(y, w_ref[...], preferred_element_type=f32)
    out_ref[0] = x_ref[0] + z * _rms_scale(z) * g_ref[...]


def _out_proj(x, oa, o1, o4, o16, l1, l4, l16, ga, gb, yc, w, g):
    b, s, d = x.shape
    tm = PROJ_ROWS
    row = lambda width: pl.BlockSpec((1, tm, width), lambda bi, i: (bi, i, 0))
    dilated = [pl.BlockSpec((1, dil, tm // dil, B_W), lambda bi, i: (bi, 0, i, 0))
               for _, dil in B_CONFIGS]
    return pl.pallas_call(
        _out_proj_kernel,
        grid=(b, s // tm),
        in_specs=[row(d), row(A_W)] + dilated + dilated + [row(A_W), row(B_W), row(C_W),
                  pl.BlockSpec((d, d), lambda bi, i: (0, 0)),
                  pl.BlockSpec((1, d), lambda bi, i: (0, 0))],
        out_specs=row(d),
        out_shape=jax.ShapeDtypeStruct((b, s, d), x.dtype),
        scratch_shapes=[pltpu.VMEM((B_W // LANES, tm, LANES), jnp.float32)] * 4,
        compiler_params=pltpu.CompilerParams(
            dimension_semantics=("parallel", "parallel"), vmem_limit_bytes=VMEM_LIMIT),
        name="out_proj",
    )(x, oa, o1, o4, o16, l1, l4, l16, ga, gb, yc, w, g)


def _rope_tables(seq):
    inv_freq = ROPE_THETA ** (-jnp.arange(0, ROT_DIM, 2, dtype=jnp.float32) / ROT_DIM)
    ang = jnp.arange(seq, dtype=jnp.float32)[:, None] * inv_freq[None, :]
    cos, sin = jnp.cos(ang), jnp.sin(ang)
    pad = HEAD_DIM - ROT_DIM
    cos_h = jnp.concatenate([cos, cos, jnp.ones((seq, pad), jnp.float32)], axis=1)
    sin_h = jnp.concatenate([-sin, sin, jnp.zeros((seq, pad), jnp.float32)], axis=1)
    reps = LANES // HEAD_DIM
    return jnp.tile(cos_h, (1, reps)), jnp.tile(sin_h, (1, reps))


def kernel(x, mem, pre_norm, w_in, sink_a, mem_norm, w_mem_kv, w_out, post_norm):
    depth = pre_norm.shape[0]
    cos_t, sin_t = _rope_tables(x.shape[1])
    bf16 = jnp.bfloat16
    h = x
    for l in range(depth):
        mk, mv = _mem_kv(mem, mem_norm[l][None], w_mem_kv[l].astype(bf16))
        qa, ka, va, ga, gb, yc, *qkv_b = _in_proj(
            h, pre_norm[l][None], w_in[l].astype(bf16), cos_t, sin_t, mk, mv)
        (oa,) = _band_attn(qa[:, None], ka[:, None], va[:, None], dil=1,
                           max_dist=A_WINDOW - 1, sink=sink_a[l])
        n_cfg = len(B_CONFIGS)
        ob = [_band_attn(qkv_b[c], qkv_b[n_cfg + c], qkv_b[2 * n_cfg + c], dil=dil,
                         max_dist=win // dil, want_lse=True)
              for c, (win, dil) in enumerate(B_CONFIGS)]
        h = _out_proj(h, oa[:, 0], ob[0][0], ob[1][0], ob[2][0], ob[0][1], ob[1][1], ob[2][1],
                      ga, gb, yc, w_out[l].astype(bf16), post_norm[l][None])
    return h
```

```python
import functools
import math

import jax
import jax.numpy as jnp
from jax import lax
from jax.experimental import pallas as pl
from jax.experimental.pallas import tpu as pltpu

HEAD_DIM = 64
ROT_DIM = HEAD_DIM // 4
ROPE_THETA = 500000.0
BLOCK = 128
C_HEADS = 4
A_Q_HEADS = 6
A_KV_HEADS = 2
A_GROUP = A_Q_HEADS // A_KV_HEADS
A_WINDOW = 128
B_HEADS = 6
B_CONFIGS = ((128, 1), (512, 4), (2048, 16))
RMS_EPS = 1e-6

LANES = 128
A_W = A_Q_HEADS * HEAD_DIM
A_KV_W = A_KV_HEADS * HEAD_DIM
B_W = B_HEADS * HEAD_DIM
C_W = C_HEADS * HEAD_DIM

IN_ROWS = 1024
PROJ_ROWS = 512
ATTN_ROWS = 2048
VMEM_LIMIT = 48 * 1024 * 1024

LOG2E = math.log2(math.e)
Q_SCALE = HEAD_DIM ** -0.5 * LOG2E


def _rms_scale(xf):
    return lax.rsqrt(jnp.mean(xf * xf, axis=-1, keepdims=True) + RMS_EPS)


def _silu(g):
    return g * (1.0 / (1.0 + jnp.exp(-g)))


def _left_half():
    return lax.broadcasted_iota(jnp.int32, (1, LANES), 1) < HEAD_DIM


def _pair_attention(q2, kg, v_ones, bias):
    rows = q2.shape[0]
    left = _left_half()
    zero = jnp.zeros_like(q2)
    qs = jnp.concatenate([jnp.where(left, q2, zero), jnp.where(left, zero, q2)], axis=0)
    s = lax.dot_general(qs, kg, (((1,), (1,)), ((), ())), preferred_element_type=jnp.float32)
    if bias is not None:
        s = s + bias
    m = jnp.max(s, axis=-1, keepdims=True)
    e = jnp.exp2((s - m).astype(jnp.bfloat16))
    pv = jnp.dot(e, v_ones, preferred_element_type=jnp.float32)
    acc = jnp.where(left, pv[:rows, :LANES], pv[rows:, :LANES])
    l = jnp.where(left, pv[:rows, LANES:], pv[rows:, LANES:])
    m_own = jnp.where(left, m[:rows], m[rows:])
    return acc * (1.0 / l), m_own + jnp.log(l) * LOG2E


def _mem_kv_kernel(mem_ref, g_ref, w_ref, mk_ref, mv_ref):
    m = mem_ref[0]
    u = (m * _rms_scale(m) * g_ref[...]).astype(jnp.bfloat16)
    kv = jnp.dot(u, w_ref[...], preferred_element_type=jnp.float32)
    mk_ref[0] = kv[:, :C_W].astype(jnp.bfloat16)
    mv_ref[0] = kv[:, C_W:].astype(jnp.bfloat16)


def _mem_kv(mem, g, w):
    b, n_mem, d = mem.shape
    return pl.pallas_call(
        _mem_kv_kernel,
        grid=(b,),
        in_specs=[
            pl.BlockSpec((1, n_mem, d), lambda i: (i, 0, 0)),
            pl.BlockSpec((1, d), lambda i: (0, 0)),
            pl.BlockSpec((d, 2 * C_W), lambda i: (0, 0)),
        ],
        out_specs=[
            pl.BlockSpec((1, n_mem, C_W), lambda i: (i, 0, 0)),
            pl.BlockSpec((1, n_mem, C_W), lambda i: (i, 0, 0)),
        ],
        out_shape=[jax.ShapeDtypeStruct((b, n_mem, C_W), jnp.bfloat16)] * 2,
        name="mem_kv",
    )(mem, g, w)


_IN_WIDTHS = (("qa", A_W), ("ka", A_KV_W), ("va", A_KV_W), ("ga", A_W),
              ("qb", B_W), ("kb", B_W), ("vb", B_W), ("gb", B_W), ("qc", C_W), ("gc", C_W))
D_IN = sum(w for _, w in _IN_WIDTHS)
_DOT_GROUPS = (("qa", "ka"), ("va", "ga"), ("qb", "kb"), ("vb", "gb"), ("qc", "gc"))


def _rope(t, cos_t, sin_t, first8):
    half = ROT_DIM // 2
    out = []
    for j in range(t.shape[1] // LANES):
        tj = t[:, j * LANES:(j + 1) * LANES]
        partner = jnp.where(first8, pltpu.roll(tj, LANES - half, 1), pltpu.roll(tj, half, 1))
        out.append(tj * cos_t + partner * sin_t)
    return out[0] if len(out) == 1 else jnp.concatenate(out, axis=1)


def _expand_kv(t):
    left = _left_half()
    swapped = pltpu.roll(t, HEAD_DIM, 1)
    groups = []
    for p in range(A_Q_HEADS // 2):
        kv_l, kv_r = (2 * p) // A_GROUP, (2 * p + 1) // A_GROUP
        if (kv_l, kv_r) == (0, 1):
            groups.append(t)
        elif kv_l == kv_r == 0:
            groups.append(jnp.where(left, t, swapped))
        else:
            groups.append(jnp.where(left, swapped, t))
    return jnp.concatenate(groups, axis=1)


def _store_dilated(t, refs, stage_ref, stage4_ref):
    rows = t.shape[0]
    slabs = t.shape[1] // LANES
    refs[1][0, 0] = t.astype(refs[1].dtype)
    for sl in range(slabs):
        stage_ref[sl] = t[:, sl * LANES:(sl + 1) * LANES]
    n4, n16 = rows // 4, rows // 16
    for sl in range(slabs):
        cols = slice(sl * LANES, (sl + 1) * LANES)
        for r in range(4):
            c4 = stage_ref[sl, pl.ds(r, n4, stride=4), :]
            refs[4][0, r, :, cols] = c4.astype(refs[4].dtype)
            stage4_ref[sl, r * n4:(r + 1) * n4, :] = c4
        for r in range(4):
            for q in range(4):
                c16 = stage4_ref[sl, pl.ds(r * n4 + q, n16, stride=4), :]
                refs[16][0, r + 4 * q, :, cols] = c16.astype(refs[16].dtype)


def _in_proj_kernel(x_ref, g_ref, w_ref, cos_ref, sin_ref, mk_ref, mv_ref,
                    qa_ref, ka_ref, va_ref, ga_ref, gb_ref, yc_ref, *rest):
    dils = [d for _, d in B_CONFIGS]
    qb_refs = dict(zip(dils, rest[0:3]))
    kb_refs = dict(zip(dils, rest[3:6]))
    vb_refs = dict(zip(dils, rest[6:9]))
    stage_ref, stage4_ref = rest[9:11]
    bf16 = jnp.bfloat16
    xf = x_ref[0]
    u = (xf * _rms_scale(xf) * g_ref[...]).astype(bf16)
    cos_t = cos_ref[...]
    sin_t = sin_ref[...]
    lane = lax.broadcasted_iota(jnp.int32, (1, LANES), 1)
    first8 = (lane % HEAD_DIM) < (ROT_DIM // 2)
    rope = lambda t: _rope(t, cos_t, sin_t, first8)

    offsets = {}
    off = 0
    for name, wd in _IN_WIDTHS:
        offsets[name] = (off, wd)
        off += wd

    def proj(names):
        c0 = offsets[names[0]][0]
        c1 = offsets[names[-1]][0] + offsets[names[-1]][1]
        full = jnp.dot(u, w_ref[:, c0:c1], preferred_element_type=jnp.float32)
        return [full[:, offsets[n][0] - c0:offsets[n][0] - c0 + offsets[n][1]] for n in names]

    qc, gc = proj(_DOT_GROUPS[4])
    qc = (qc * Q_SCALE).astype(bf16)
    ones = jnp.ones((mv_ref.shape[1], LANES), bf16)
    oc = []
    for p in range(C_W // LANES):
        cols = slice(p * LANES, (p + 1) * LANES)
        v_ones = jnp.concatenate([mv_ref[0, :, cols], ones], axis=1)
        o, _ = _pair_attention(qc[:, cols], mk_ref[0, :, cols], v_ones, None)
        oc.append(o)
    yc_ref[0] = (jnp.concatenate(oc, axis=1) * _silu(gc)).astype(bf16)

    qb, kb = proj(_DOT_GROUPS[2])
    _store_dilated(rope(qb) * Q_SCALE, qb_refs, stage_ref, stage4_ref)
    _store_dilated(rope(kb), kb_refs, stage_ref, stage4_ref)
    vb, gb = proj(_DOT_GROUPS[3])
    _store_dilated(vb, vb_refs, stage_ref, stage4_ref)
    gb_ref[0] = _silu(gb).astype(bf16)
    qa, ka = proj(_DOT_GROUPS[0])
    qa_ref[0] = (rope(qa) * Q_SCALE).astype(bf16)
    ka_ref[0] = _expand_kv(rope(ka)).astype(bf16)
    va, ga = proj(_DOT_GROUPS[1])
    va_ref[0] = _expand_kv(va).astype(bf16)
    ga_ref[0] = _silu(ga).astype(bf16)


def _in_proj(x, g, w, cos_t, sin_t, mk, mv):
    b, s, d = x.shape
    tm = IN_ROWS
    n_mem = mk.shape[1]
    row = lambda width: pl.BlockSpec((1, tm, width), lambda i, bi: (bi, i, 0))
    out_widths = (A_W, A_W, A_W, A_W, B_W, C_W)
    out_specs = [row(wd) for wd in out_widths]
    out_shape = [jax.ShapeDtypeStruct((b, s, wd), jnp.bfloat16) for wd in out_widths]
    for _ in range(3):
        for _, dil in B_CONFIGS:
            out_specs.append(pl.BlockSpec((1, dil, tm // dil, B_W), lambda i, bi: (bi, 0, i, 0)))
            out_shape.append(jax.ShapeDtypeStruct((b, dil, s // dil, B_W), jnp.bfloat16))
    stage = pltpu.VMEM((B_W // LANES, tm, LANES), jnp.float32)
    return pl.pallas_call(
        _in_proj_kernel,
        grid=(s // tm, b),
        in_specs=[
            row(d),
            pl.BlockSpec((1, d), lambda i, bi: (0, 0)),
            pl.BlockSpec((d, D_IN), lambda i, bi: (0, 0)),
            pl.BlockSpec((tm, LANES), lambda i, bi: (i, 0)),
            pl.BlockSpec((tm, LANES), lambda i, bi: (i, 0)),
            pl.BlockSpec((1, n_mem, C_W), lambda i, bi: (bi, 0, 0)),
            pl.BlockSpec((1, n_mem, C_W), lambda i, bi: (bi, 0, 0)),
        ],
        out_specs=out_specs,
        out_shape=out_shape,
        scratch_shapes=[stage, stage],
        compiler_params=pltpu.CompilerParams(
            dimension_semantics=("parallel", "parallel"), vmem_limit_bytes=VMEM_LIMIT),
        name="in_proj",
    )(x, g, w, cos_t, sin_t, mk, mv)


def _band_attn_kernel(*refs, max_dist, has_sink, want_lse):
    refs = list(refs)
    sink_ref = refs.pop(0) if has_sink else None
    q_ref, kp_ref, ko_ref, vp_ref, vo_ref, o_ref = refs[:6]
    lse_ref = refs[6] if want_lse else None
    rows = q_ref.shape[2]
    groups = q_ref.shape[3] // LANES

    qi = lax.broadcasted_iota(jnp.int32, (BLOCK, 2 * BLOCK), 0)
    kc = lax.broadcasted_iota(jnp.int32, (BLOCK, 2 * BLOCK), 1)
    dist = qi + BLOCK - kc
    band = (dist >= 0) & (dist <= max_dist)
    first_key = jnp.where(pl.program_id(2) == 0, BLOCK, 0)
    neg = jnp.float32(-jnp.inf)
    bias_any = jnp.where(band, 0.0, neg)
    bias_first = jnp.where(band & (kc >= first_key), 0.0, neg)
    ones = jnp.ones((2 * BLOCK, LANES), jnp.bfloat16)

    def pair_bias(base, p):
        if not has_sink:
            return jnp.concatenate([base, base], axis=0)
        assert max_dist < BLOCK
        halves = [jnp.where(kc == 0, sink_ref[2 * p + h] * LOG2E, base) for h in range(2)]
        return jnp.concatenate(halves, axis=0)

    bias_rest = [pair_bias(bias_any, p) for p in range(groups)]
    bias_0 = [pair_bias(bias_first, p) for p in range(groups)]
    row0 = lax.broadcasted_iota(jnp.int32, (BLOCK, 1), 0) == 0

    for c in range(q_ref.shape[1]):
        for i in range(rows // BLOCK):
            r = slice(i * BLOCK, (i + 1) * BLOCK)
            q = q_ref[0, c, r, :]
            if i == 0:
                k_prev, v_prev = kp_ref[0, c], vp_ref[0, c]
            else:
                before = slice((i - 1) * BLOCK, i * BLOCK)
                k_prev, v_prev = ko_ref[0, c, before, :], vo_ref[0, c, before, :]
            if has_sink:
                k_prev = jnp.where(row0, jnp.zeros_like(k_prev), k_prev)
                v_prev = jnp.where(row0, jnp.zeros_like(v_prev), v_prev)
            k = jnp.concatenate([k_prev, ko_ref[0, c, r, :]], axis=0)
            v = jnp.concatenate([v_prev, vo_ref[0, c, r, :]], axis=0)
            for p in range(groups):
                cols = slice(p * LANES, (p + 1) * LANES)
                v_ones = jnp.concatenate([v[:, cols], ones], axis=1)
                o, lse2 = _pair_attention(q[:, cols], k[:, cols], v_ones,
                                          bias_0[p] if i == 0 else bias_rest[p])
                o_ref[0, c, r, cols] = o.astype(o_ref.dtype)
                if want_lse:
                    lse_ref[0, c, r, cols] = lse2


def _band_attn(q, k, v, *, dil, max_dist, sink=None, want_lse=False):
    b, n_seq, seq, wq = q.shape
    rows = min(ATTN_ROWS, seq)
    sub = rows // BLOCK
    seqs = min(ATTN_ROWS // rows, n_seq)
    own = lambda n: pl.BlockSpec((1, seqs, n, wq), lambda bi, r, j: (bi, r, j, 0))
    prev = pl.BlockSpec(
        (1, seqs, BLOCK, wq), lambda bi, r, j: (bi, r, jnp.maximum(j * sub - 1, 0), 0))
    in_specs = [own(rows), prev, own(rows), prev, own(rows)]
    args = [q, k, k, v, v]
    if sink is not None:
        in_specs.insert(0, pl.BlockSpec(memory_space=pltpu.SMEM))
        args.insert(0, sink)
    out_specs = [own(rows)]
    out_shape = [jax.ShapeDtypeStruct(q.shape, jnp.bfloat16)]
    if want_lse:
        out_specs.append(own(rows))
        out_shape.append(jax.ShapeDtypeStruct(q.shape, jnp.float32))
    return pl.pallas_call(
        functools.partial(_band_attn_kernel, max_dist=max_dist,
                          has_sink=sink is not None, want_lse=want_lse),
        grid=(b, n_seq // seqs, seq // rows),
        in_specs=in_specs,
        out_specs=out_specs,
        out_shape=out_shape,
        compiler_params=pltpu.CompilerParams(
            dimension_semantics=("parallel", "parallel", "parallel"),
            vmem_limit_bytes=VMEM_LIMIT),
        name=f"band_attn_{'a' if sink is not None else 'b'}{dil}",
    )(*args)


def _natural_rows(ref, stage_ref):
    dil, n = ref.shape[1], ref.shape[2]
    if dil == 1:
        return ref[0, 0].astype(jnp.float32)
    slabs = ref.shape[3] // LANES
    for r in range(dil):
        t = ref[0, r].astype(jnp.float32)
        for sl in range(slabs):
            stage_ref[sl, pl.ds(r, n, stride=dil), :] = t[:, sl * LANES:(sl + 1) * LANES]
    return jnp.concatenate([stage_ref[sl] for sl in range(slabs)], axis=1)


def _out_proj_kernel(x_ref, oa_ref, o1_ref, o4_ref, o16_ref, l1_ref, l4_ref, l16_ref,
                     ga_ref, gb_ref, yc_ref, w_ref, g_ref, out_ref, *stage):
    f32 = jnp.float32
    l1 = _natural_rows(l1_ref, None)
    l4 = _natural_rows(l4_ref, stage[0])
    l16 = _natural_rows(l16_ref, stage[1])
    mx = jnp.maximum(jnp.maximum(l1, l4), l16)
    e1, e4, e16 = jnp.exp2(l1 - mx), jnp.exp2(l4 - mx), jnp.exp2(l16 - mx)
    den = e1 + e4 + e16
    ob = (e1 * _natural_rows(o1_ref, None) + e4 * _natural_rows(o4_ref, stage[2])
          + e16 * _natural_rows(o16_ref, stage[3])) * (1.0 / den)
    ya = oa_ref[0].astype(f32) * ga_ref[0].astype(f32)
    yb = ob * gb_ref[0].astype(f32)
    y = jnp.concatenate([ya.astype(jnp.bfloat16), yb.astype(jnp.bfloat16), yc_ref[0]], axis=1)
    z = jnp.dot(y, w_ref[...], preferred_element_type=f32)
    out_ref[0] = x_ref[0] + z * _rms_scale(z) * g_ref[...]


def _out_proj(x, oa, o1, o4, o16, l1, l4, l16, ga, gb, yc, w, g):
    b, s, d = x.shape
    tm = PROJ_ROWS
    row = lambda width: pl.BlockSpec((1, tm, width), lambda bi, i: (bi, i, 0))
    dilated = [pl.BlockSpec((1, dil, tm // dil, B_W), lambda bi, i: (bi, 0, i, 0))
               for _, dil in B_CONFIGS]
    return pl.pallas_call(
        _out_proj_kernel,
        grid=(b, s // tm),
        in_specs=[row(d), row(A_W)] + dilated + dilated + [row(A_W), row(B_W), row(C_W),
                  pl.BlockSpec((d, d), lambda bi, i: (0, 0)),
                  pl.BlockSpec((1, d), lambda bi, i: (0, 0))],
        out_specs=row(d),
        out_shape=jax.ShapeDtypeStruct((b, s, d), x.dtype),
        scratch_shapes=[pltpu.VMEM((B_W // LANES, tm, LANES), jnp.float32)] * 4,
        compiler_params=pltpu.CompilerParams(
            dimension_semantics=("parallel", "parallel"), vmem_limit_bytes=VMEM_LIMIT),
        name="out_proj",
    )(x, oa, o1, o4, o16, l1, l4, l16, ga, gb, yc, w, g)


def _rope_tables(seq):
    inv_freq = ROPE_THETA ** (-jnp.arange(0, ROT_DIM, 2, dtype=jnp.float32) / ROT_DIM)
    ang = jnp.arange(seq, dtype=jnp.float32)[:, None] * inv_freq[None, :]
    cos, sin = jnp.cos(ang), jnp.sin(ang)
    pad = HEAD_DIM - ROT_DIM
    cos_h = jnp.concatenate([cos, cos, jnp.ones((seq, pad), jnp.float32)], axis=1)
    sin_h = jnp.concatenate([-sin, sin, jnp.zeros((seq, pad), jnp.float32)], axis=1)
    reps = LANES // HEAD_DIM
    return jnp.tile(cos_h, (1, reps)), jnp.tile(sin_h, (1, reps))


def kernel(x, mem, pre_norm, w_in, sink_a, mem_norm, w_mem_kv, w_out, post_norm):
    depth = pre_norm.shape[0]
    cos_t, sin_t = _rope_tables(x.shape[1])
    bf16 = jnp.bfloat16
    h = x
    for l in range(depth):
        mk, mv = _mem_kv(mem, mem_norm[l][None], w_mem_kv[l].astype(bf16))
        qa, ka, va, ga, gb, yc, *qkv_b = _in_proj(
            h, pre_norm[l][None], w_in[l].astype(bf16), cos_t, sin_t, mk, mv)
        (oa,) = _band_attn(qa[:, None], ka[:, None], va[:, None], dil=1,
                           max_dist=A_WINDOW - 1, sink=sink_a[l])
        n_cfg = len(B_CONFIGS)
        ob = [_band_attn(qkv_b[c], qkv_b[n_cfg + c], qkv_b[2 * n_cfg + c], dil=dil,
                         max_dist=win // dil, want_lse=True)
              for c, (win, dil) in enumerate(B_CONFIGS)]
        h = _out_proj(h, oa[:, 0], ob[0][0], ob[1][0], ob[2][0], ob[0][1], ob[1][1], ob[2][1],
                      ga, gb, yc, w_out[l].astype(bf16), post_norm[l][None])
    return h
```

```python
import functools
import math

import jax
import jax.numpy as jnp
from jax import lax
from jax.experimental import pallas as pl
from jax.experimental.pallas import tpu as pltpu

HEAD_DIM = 64
ROT_DIM = HEAD_DIM // 4
ROPE_THETA = 500000.0
BLOCK = 128
C_HEADS = 4
A_Q_HEADS = 6
A_KV_HEADS = 2
A_GROUP = A_Q_HEADS // A_KV_HEADS
A_WINDOW = 128
B_HEADS = 6
B_CONFIGS = ((128, 1), (512, 4), (2048, 16))
RMS_EPS = 1e-6

LANES = 128
A_W = A_Q_HEADS * HEAD_DIM
A_KV_W = A_KV_HEADS * HEAD_DIM
B_W = B_HEADS * HEAD_DIM
C_W = C_HEADS * HEAD_DIM

IN_ROWS = 1024
PROJ_ROWS = 512
ATTN_ROWS = 2048
VMEM_LIMIT = 48 * 1024 * 1024
ATTN_B_VMEM_LIMIT = 56 * 1024 * 1024

LOG2E = math.log2(math.e)
Q_SCALE = HEAD_DIM ** -0.5 * LOG2E


def _rms_scale(xf):
    return lax.rsqrt(jnp.mean(xf * xf, axis=-1, keepdims=True) + RMS_EPS)


def _silu(g):
    return g * (1.0 / (1.0 + jnp.exp(-g)))


def _left_half():
    return lax.broadcasted_iota(jnp.int32, (1, LANES), 1) < HEAD_DIM


def _pair_attention(q2, kg, v_ones, bias):
    rows = q2.shape[0]
    left = _left_half()
    zero = jnp.zeros_like(q2)
    qs = jnp.concatenate([jnp.where(left, q2, zero), jnp.where(left, zero, q2)], axis=0)
    s = lax.dot_general(qs, kg, (((1,), (1,)), ((), ())), preferred_element_type=jnp.float32)
    if bias is not None:
        s = s + bias
    m = jnp.max(s, axis=-1, keepdims=True)
    e = jnp.exp2((s - m).astype(jnp.bfloat16))
    pv = jnp.dot(e, v_ones, preferred_element_type=jnp.float32)
    acc = jnp.where(left, pv[:rows, :LANES], pv[rows:, :LANES])
    l = jnp.where(left, pv[:rows, LANES:], pv[rows:, LANES:])
    m_own = jnp.where(left, m[:rows], m[rows:])
    return acc * (1.0 / l), m_own + jnp.log(l) * LOG2E


def _mem_kv_kernel(mem_ref, g_ref, w_ref, mk_ref, mv_ref):
    m = mem_ref[0]
    u = (m * _rms_scale(m) * g_ref[...]).astype(jnp.bfloat16)
    kv = jnp.dot(u, w_ref[...], preferred_element_type=jnp.float32)
    mk_ref[0] = kv[:, :C_W].astype(jnp.bfloat16)
    mv_ref[0] = kv[:, C_W:].astype(jnp.bfloat16)


def _mem_kv(mem, g, w):
    b, n_mem, d = mem.shape
    return pl.pallas_call(
        _mem_kv_kernel,
        grid=(b,),
        in_specs=[
            pl.BlockSpec((1, n_mem, d), lambda i: (i, 0, 0)),
            pl.BlockSpec((1, d), lambda i: (0, 0)),
            pl.BlockSpec((d, 2 * C_W), lambda i: (0, 0)),
        ],
        out_specs=[
            pl.BlockSpec((1, n_mem, C_W), lambda i: (i, 0, 0)),
            pl.BlockSpec((1, n_mem, C_W), lambda i: (i, 0, 0)),
        ],
        out_shape=[jax.ShapeDtypeStruct((b, n_mem, C_W), jnp.bfloat16)] * 2,
        name="mem_kv",
    )(mem, g, w)


_IN_WIDTHS = (("qa", A_W), ("ka", A_KV_W), ("va", A_KV_W), ("ga", A_W),
              ("qb", B_W), ("kb", B_W), ("vb", B_W), ("gb", B_W), ("qc", C_W), ("gc", C_W))
D_IN = sum(w for _, w in _IN_WIDTHS)
_DOT_GROUPS = (("qa", "ka"), ("va", "ga"), ("qb", "kb"), ("vb", "gb"), ("qc", "gc"))


def _rope(t, cos_t, sin_t, first8):
    half = ROT_DIM // 2
    out = []
    for j in range(t.shape[1] // LANES):
        tj = t[:, j * LANES:(j + 1) * LANES]
        partner = jnp.where(first8, pltpu.roll(tj, LANES - half, 1), pltpu.roll(tj, half, 1))
        out.append(tj * cos_t + partner * sin_t)
    return out[0] if len(out) == 1 else jnp.concatenate(out, axis=1)


def _expand_kv(t):
    left = _left_half()
    swapped = pltpu.roll(t, HEAD_DIM, 1)
    groups = []
    for p in range(A_Q_HEADS // 2):
        kv_l, kv_r = (2 * p) // A_GROUP, (2 * p + 1) // A_GROUP
        if (kv_l, kv_r) == (0, 1):
            groups.append(t)
        elif kv_l == kv_r == 0:
            groups.append(jnp.where(left, t, swapped))
        else:
            groups.append(jnp.where(left, swapped, t))
    return jnp.concatenate(groups, axis=1)


def _store_dilated(t, refs, stage_ref, stage4_ref):
    rows = t.shape[0]
    slabs = t.shape[1] // LANES
    refs[1][0, 0] = t.astype(refs[1].dtype)
    for sl in range(slabs):
        stage_ref[sl] = t[:, sl * LANES:(sl + 1) * LANES]
    n4, n16 = rows // 4, rows // 16
    for sl in range(slabs):
        cols = slice(sl * LANES, (sl + 1) * LANES)
        for r in range(4):
            c4 = stage_ref[sl, pl.ds(r, n4, stride=4), :]
            refs[4][0, r, :, cols] = c4.astype(refs[4].dtype)
            stage4_ref[sl, r * n4:(r + 1) * n4, :] = c4
        for r in range(4):
            for q in range(4):
                c16 = stage4_ref[sl, pl.ds(r * n4 + q, n16, stride=4), :]
                refs[16][0, r + 4 * q, :, cols] = c16.astype(refs[16].dtype)


def _in_proj_kernel(x_ref, g_ref, w_ref, cos_ref, sin_ref, mk_ref, mv_ref,
                    qa_ref, ka_ref, va_ref, ga_ref, gb_ref, yc_ref, *rest):
    dils = [d for _, d in B_CONFIGS]
    qb_refs = dict(zip(dils, rest[0:3]))
    kb_refs = dict(zip(dils, rest[3:6]))
    vb_refs = dict(zip(dils, rest[6:9]))
    stage_ref, stage4_ref = rest[9:11]
    bf16 = jnp.bfloat16
    xf = x_ref[0]
    u = (xf * _rms_scale(xf) * g_ref[...]).astype(bf16)
    cos_t = cos_ref[...]
    sin_t = sin_ref[...]
    lane = lax.broadcasted_iota(jnp.int32, (1, LANES), 1)
    first8 = (lane % HEAD_DIM) < (ROT_DIM // 2)
    rope = lambda t: _rope(t, cos_t, sin_t, first8)

    offsets = {}
    off = 0
    for name, wd in _IN_WIDTHS:
        offsets[name] = (off, wd)
        off += wd

    def proj(names):
        c0 = offsets[names[0]][0]
        c1 = offsets[names[-1]][0] + offsets[names[-1]][1]
        full = jnp.dot(u, w_ref[:, c0:c1], preferred_element_type=jnp.float32)
        return [full[:, offsets[n][0] - c0:offsets[n][0] - c0 + offsets[n][1]] for n in names]

    qc, gc = proj(_DOT_GROUPS[4])
    qc = (qc * Q_SCALE).astype(bf16)
    ones = jnp.ones((mv_ref.shape[1], LANES), bf16)
    oc = []
    for p in range(C_W // LANES):
        cols = slice(p * LANES, (p + 1) * LANES)
        v_ones = jnp.concatenate([mv_ref[0, :, cols], ones], axis=1)
        o, _ = _pair_attention(qc[:, cols], mk_ref[0, :, cols], v_ones, None)
        oc.append(o)
    yc_ref[0] = (jnp.concatenate(oc, axis=1) * _silu(gc)).astype(bf16)

    qb, kb = proj(_DOT_GROUPS[2])
    _store_dilated(rope(qb) * Q_SCALE, qb_refs, stage_ref, stage4_ref)
    _store_dilated(rope(kb), kb_refs, stage_ref, stage4_ref)
    vb, gb = proj(_DOT_GROUPS[3])
    _store_dilated(vb, vb_refs, stage_ref, stage4_ref)
    gb_ref[0] = _silu(gb).astype(bf16)
    qa, ka = proj(_DOT_GROUPS[0])
    qa_ref[0] = (rope(qa) * Q_SCALE).astype(bf16)
    ka_ref[0] = _expand_kv(rope(ka)).astype(bf16)
    va, ga = proj(_DOT_GROUPS[1])
    va_ref[0] = _expand_kv(va).astype(bf16)
    ga_ref[0] = _silu(ga).astype(bf16)


def _in_proj(x, g, w, cos_t, sin_t, mk, mv):
    b, s, d = x.shape
    tm = IN_ROWS
    n_mem = mk.shape[1]
    row = lambda width: pl.BlockSpec((1, tm, width), lambda i, bi: (bi, i, 0))
    out_widths = (A_W, A_W, A_W, A_W, B_W, C_W)
    out_specs = [row(wd) for wd in out_widths]
    out_shape = [jax.ShapeDtypeStruct((b, s, wd), jnp.bfloat16) for wd in out_widths]
    for _ in range(3):
        for _, dil in B_CONFIGS:
            out_specs.append(pl.BlockSpec((1, dil, tm // dil, B_W), lambda i, bi: (bi, 0, i, 0)))
            out_shape.append(jax.ShapeDtypeStruct((b, dil, s // dil, B_W), jnp.bfloat16))
    stage = pltpu.VMEM((B_W // LANES, tm, LANES), jnp.float32)
    return pl.pallas_call(
        _in_proj_kernel,
        grid=(s // tm, b),
        in_specs=[
            row(d),
            pl.BlockSpec((1, d), lambda i, bi: (0, 0)),
            pl.BlockSpec((d, D_IN), lambda i, bi: (0, 0)),
            pl.BlockSpec((tm, LANES), lambda i, bi: (i, 0)),
            pl.BlockSpec((tm, LANES), lambda i, bi: (i, 0)),
            pl.BlockSpec((1, n_mem, C_W), lambda i, bi: (bi, 0, 0)),
            pl.BlockSpec((1, n_mem, C_W), lambda i, bi: (bi, 0, 0)),
        ],
        out_specs=out_specs,
        out_shape=out_shape,
        scratch_shapes=[stage, stage],
        compiler_params=pltpu.CompilerParams(
            dimension_semantics=("parallel", "parallel"), vmem_limit_bytes=VMEM_LIMIT),
        name="in_proj",
    )(x, g, w, cos_t, sin_t, mk, mv)


def _band_biases(max_dist, first_step, sink_ref, groups):
    qi = lax.broadcasted_iota(jnp.int32, (BLOCK, 2 * BLOCK), 0)
    kc = lax.broadcasted_iota(jnp.int32, (BLOCK, 2 * BLOCK), 1)
    dist = qi + BLOCK - kc
    band = (dist >= 0) & (dist <= max_dist)
    first_key = jnp.where(first_step, BLOCK, 0)
    neg = jnp.float32(-jnp.inf)
    bias_any = jnp.where(band, 0.0, neg)
    bias_first = jnp.where(band & (kc >= first_key), 0.0, neg)

    def pair_bias(base, p):
        if sink_ref is None:
            return jnp.concatenate([base, base], axis=0)
        assert max_dist < BLOCK
        halves = [jnp.where(kc == 0, sink_ref[2 * p + h] * LOG2E, base) for h in range(2)]
        return jnp.concatenate(halves, axis=0)

    if sink_ref is None:
        first, rest = pair_bias(bias_first, 0), pair_bias(bias_any, 0)
        return [first] * groups, [rest] * groups
    return ([pair_bias(bias_first, p) for p in range(groups)],
            [pair_bias(bias_any, p) for p in range(groups)])


def _band_chains(q_ref, kp_ref, ko_ref, vp_ref, vo_ref, bias_0, bias_rest, zero_row0, emit):
    rows = q_ref.shape[2]
    groups = q_ref.shape[3] // LANES
    ones = jnp.ones((2 * BLOCK, LANES), jnp.bfloat16)
    row0 = lax.broadcasted_iota(jnp.int32, (BLOCK, 1), 0) == 0
    for c in range(q_ref.shape[1]):
        for i in range(rows // BLOCK):
            r = slice(i * BLOCK, (i + 1) * BLOCK)
            q = q_ref[0, c, r, :]
            if i == 0:
                k_prev, v_prev = kp_ref[0, c], vp_ref[0, c]
            else:
                before = slice((i - 1) * BLOCK, i * BLOCK)
                k_prev, v_prev = ko_ref[0, c, before, :], vo_ref[0, c, before, :]
            if zero_row0:
                k_prev = jnp.where(row0, jnp.zeros_like(k_prev), k_prev)
                v_prev = jnp.where(row0, jnp.zeros_like(v_prev), v_prev)
            k = jnp.concatenate([k_prev, ko_ref[0, c, r, :]], axis=0)
            v = jnp.concatenate([v_prev, vo_ref[0, c, r, :]], axis=0)
            for p in range(groups):
                cols = slice(p * LANES, (p + 1) * LANES)
                v_ones = jnp.concatenate([v[:, cols], ones], axis=1)
                o, lse2 = _pair_attention(q[:, cols], k[:, cols], v_ones,
                                          bias_0[p] if i == 0 else bias_rest[p])
                emit(c, i, p, o, lse2)


def _qkv_specs(dil, rows, width):
    n = rows // dil
    own = pl.BlockSpec((1, dil, n, width), lambda bi, j: (bi, 0, j, 0))
    prev = pl.BlockSpec(
        (1, dil, BLOCK, width), lambda bi, j: (bi, 0, jnp.maximum(j * (n // BLOCK) - 1, 0), 0))
    return [own, prev, own, prev, own]


def _attn_a_kernel(sink_ref, q_ref, kp_ref, ko_ref, vp_ref, vo_ref, gate_ref, y_ref):
    bias_0, bias_rest = _band_biases(A_WINDOW - 1, pl.program_id(1) == 0, sink_ref,
                                     q_ref.shape[3] // LANES)

    def emit(c, i, p, o, lse2):
        r = slice(i * BLOCK, (i + 1) * BLOCK)
        cols = slice(p * LANES, (p + 1) * LANES)
        y_ref[0, r, cols] = (o * gate_ref[0, r, cols].astype(jnp.float32)).astype(y_ref.dtype)

    _band_chains(q_ref, kp_ref, ko_ref, vp_ref, vo_ref, bias_0, bias_rest, True, emit)


def _attn_a(q, k, v, gate, sink):
    b, _, s, wq = q.shape
    rows = ATTN_ROWS
    row = pl.BlockSpec((1, rows, wq), lambda bi, j: (bi, j, 0))
    return pl.pallas_call(
        _attn_a_kernel,
        grid=(b, s // rows),
        in_specs=[pl.BlockSpec(memory_space=pltpu.SMEM)] + _qkv_specs(1, rows, wq) + [row],
        out_specs=row,
        out_shape=jax.ShapeDtypeStruct((b, s, wq), jnp.bfloat16),
        compiler_params=pltpu.CompilerParams(
            dimension_semantics=("parallel", "parallel"), vmem_limit_bytes=VMEM_LIMIT),
        name="attn_a",
    )(sink, q, k, k, v, v, gate)


def _attn_b_kernel(*refs):
    n_cfg = len(B_CONFIGS)
    qkv = [refs[5 * c:5 * c + 5] for c in range(n_cfg)]
    gate_ref, y_ref = refs[5 * n_cfg:5 * n_cfg + 2]
    stage = refs[5 * n_cfg + 2:]
    first_step = pl.program_id(1) == 0
    groups = B_W // LANES
    staged = {}
    for c, (win, dil) in enumerate(B_CONFIGS):
        if dil == 1:
            continue
        o_stage, l_stage = stage[2 * (c - 1)], stage[2 * (c - 1) + 1]
        staged[dil] = (o_stage, l_stage)

        def emit(cls, i, p, o, lse2, dil=dil, o_stage=o_stage, l_stage=l_stage):
            rows = pl.ds(dil * BLOCK * i + cls, BLOCK, stride=dil)
            o_stage[p, rows, :] = o
            l_stage[p, rows, :] = lse2

        bias_0, bias_rest = _band_biases(win // dil, first_step, None, groups)
        _band_chains(*qkv[c], bias_0, bias_rest, False, emit)

    def emit_mixed(cls, i, p, o, lse2):
        r = slice(i * BLOCK, (i + 1) * BLOCK)
        cols = slice(p * LANES, (p + 1) * LANES)
        outs, lses = [o], [lse2]
        for dil in sorted(staged):
            outs.append(staged[dil][0][p, r, :])
            lses.append(staged[dil][1][p, r, :])
        mx = functools.reduce(jnp.maximum, lses)
        wts = [jnp.exp2(l - mx) for l in lses]
        num = sum(w * ov for w, ov in zip(wts[1:], outs[1:])) + wts[0] * outs[0]
        ob = num * (1.0 / sum(wts[1:], wts[0]))
        y_ref[0, r, cols] = (ob * gate_ref[0, r, cols].astype(jnp.float32)).astype(y_ref.dtype)

    win, dil = B_CONFIGS[0]
    assert dil == 1
    bias_0, bias_rest = _band_biases(win // dil, first_step, None, groups)
    _band_chains(*qkv[0], bias_0, bias_rest, False, emit_mixed)


def _attn_b(qkv_b, gate):
    n_cfg = len(B_CONFIGS)
    b, _, s, wq = qkv_b[0].shape
    rows = ATTN_ROWS
    in_specs, args = [], []
    for c, (_, dil) in enumerate(B_CONFIGS):
        q, k, v = qkv_b[c], qkv_b[n_cfg + c], qkv_b[2 * n_cfg + c]
        in_specs += _qkv_specs(dil, rows, wq)
        args += [q, k, k, v, v]
    row = pl.BlockSpec((1, rows, wq), lambda bi, j: (bi, j, 0))
    stage = pltpu.VMEM((wq // LANES, rows, LANES), jnp.float32)
    return pl.pallas_call(
        _attn_b_kernel,
        grid=(b, s // rows),
        in_specs=in_specs + [row],
        out_specs=row,
        out_shape=jax.ShapeDtypeStruct((b, s, wq), jnp.bfloat16),
        scratch_shapes=[stage] * (2 * (n_cfg - 1)),
        compiler_params=pltpu.CompilerParams(
            dimension_semantics=("parallel", "parallel"), vmem_limit_bytes=ATTN_B_VMEM_LIMIT),
        name="attn_b",
    )(*args, gate)


def _out_proj_kernel(x_ref, ya_ref, yb_ref, yc_ref, w_ref, g_ref, out_ref):
    y = jnp.concatenate([ya_ref[0], yb_ref[0], yc_ref[0]], axis=1)
    z = jnp.dot(y, w_ref[...], preferred_element_type=jnp.float32)
    out_ref[0] = x_ref[0] + z * _rms_scale(z) * g_ref[...]


def _out_proj(x, ya, yb, yc, w, g):
    b, s, d = x.shape
    tm = PROJ_ROWS
    row = lambda width: pl.BlockSpec((1, tm, width), lambda bi, i: (bi, i, 0))
    return pl.pallas_call(
        _out_proj_kernel,
        grid=(b, s // tm),
        in_specs=[row(d), row(A_W), row(B_W), row(C_W),
                  pl.BlockSpec((d, d), lambda bi, i: (0, 0)),
                  pl.BlockSpec((1, d), lambda bi, i: (0, 0))],
        out_specs=row(d),
        out_shape=jax.ShapeDtypeStruct((b, s, d), x.dtype),
        compiler_params=pltpu.CompilerParams(
            dimension_semantics=("parallel", "parallel"), vmem_limit_bytes=VMEM_LIMIT),
        name="out_proj",
    )(x, ya, yb, yc, w, g)


def _rope_tables(seq):
    inv_freq = ROPE_THETA ** (-jnp.arange(0, ROT_DIM, 2, dtype=jnp.float32) / ROT_DIM)
    ang = jnp.arange(seq, dtype=jnp.float32)[:, None] * inv_freq[None, :]
    cos, sin = jnp.cos(ang), jnp.sin(ang)
    pad = HEAD_DIM - ROT_DIM
    cos_h = jnp.concatenate([cos, cos, jnp.ones((seq, pad), jnp.float32)], axis=1)
    sin_h = jnp.concatenate([-sin, sin, jnp.zeros((seq, pad), jnp.float32)], axis=1)
    reps = LANES // HEAD_DIM
    return jnp.tile(cos_h, (1, reps)), jnp.tile(sin_h, (1, reps))


def kernel(x, mem, pre_norm, w_in, sink_a, mem_norm, w_mem_kv, w_out, post_norm):
    depth = pre_norm.shape[0]
    cos_t, sin_t = _rope_tables(x.shape[1])
    bf16 = jnp.bfloat16
    h = x
    for l in range(depth):
        mk, mv = _mem_kv(mem, mem_norm[l][None], w_mem_kv[l].astype(bf16))
        qa, ka, va, ga, gb, yc, *qkv_b = _in_proj(
            h, pre_norm[l][None], w_in[l].astype(bf16), cos_t, sin_t, mk, mv)
        ya = _attn_a(qa[:, None], ka[:, None], va[:, None], ga, sink_a[l])
        yb = _attn_b(qkv_b, gb)
        h = _out_proj(h, ya, yb, yc, w_out[l].astype(bf16), post_norm[l][None])
    return h
```

```python
import functools
import math

import jax
import jax.numpy as jnp
from jax import lax
from jax.experimental import pallas as pl
from jax.experimental.pallas import tpu as pltpu

HEAD_DIM = 64
ROT_DIM = HEAD_DIM // 4
ROPE_THETA = 500000.0
BLOCK = 128
C_HEADS = 4
A_Q_HEADS = 6
A_KV_HEADS = 2
A_GROUP = A_Q_HEADS // A_KV_HEADS
A_WINDOW = 128
B_HEADS = 6
B_CONFIGS = ((128, 1), (512, 4), (2048, 16))
RMS_EPS = 1e-6

LANES = 128
A_W = A_Q_HEADS * HEAD_DIM
A_KV_W = A_KV_HEADS * HEAD_DIM
B_W = B_HEADS * HEAD_DIM
C_W = C_HEADS * HEAD_DIM

IN_ROWS = 1024
PROJ_ROWS = 1024
ATTN_ROWS = 2048
VMEM_LIMIT = 48 * 1024 * 1024
ATTN_B_VMEM_LIMIT = 56 * 1024 * 1024

LOG2E = math.log2(math.e)
Q_SCALE = HEAD_DIM ** -0.5 * LOG2E


def _rms_scale(xf):
    return lax.rsqrt(jnp.mean(xf * xf, axis=-1, keepdims=True) + RMS_EPS)


def _silu(g):
    return g * (1.0 / (1.0 + jnp.exp(-g)))


def _left_half():
    return lax.broadcasted_iota(jnp.int32, (1, LANES), 1) < HEAD_DIM


def _pair_attention(q2, kg, v_ones, bias):
    rows = q2.shape[0]
    left = _left_half()
    keep_l = jnp.where(left, 1.0, 0.0).astype(q2.dtype)
    qs = jnp.concatenate([q2 * keep_l, q2 * (1 - keep_l)], axis=0)
    s = lax.dot_general(qs, kg, (((1,), (1,)), ((), ())), preferred_element_type=jnp.float32)
    if bias is not None:
        s = s + bias
    m = jnp.max(s, axis=-1, keepdims=True)
    e = jnp.exp2((s - m).astype(jnp.bfloat16))
    pv = jnp.dot(e, v_ones, preferred_element_type=jnp.float32)
    acc = jnp.where(left, pv[:rows, :LANES], pv[rows:, :LANES])
    l = jnp.where(left, pv[:rows, LANES:], pv[rows:, LANES:])
    m_own = jnp.where(left, m[:rows], m[rows:])
    return acc * (1.0 / l), m_own + jnp.log(l) * LOG2E


def _mem_kv_kernel(mem_ref, g_ref, w_ref, mk_ref, mv_ref):
    m = mem_ref[0]
    u = (m * _rms_scale(m) * g_ref[...]).astype(jnp.bfloat16)
    kv = jnp.dot(u, w_ref[...], preferred_element_type=jnp.float32)
    mk_ref[0] = kv[:, :C_W].astype(jnp.bfloat16)
    mv_ref[0] = kv[:, C_W:].astype(jnp.bfloat16)


def _mem_kv(mem, g, w):
    b, n_mem, d = mem.shape
    return pl.pallas_call(
        _mem_kv_kernel,
        grid=(b,),
        in_specs=[
            pl.BlockSpec((1, n_mem, d), lambda i: (i, 0, 0)),
            pl.BlockSpec((1, d), lambda i: (0, 0)),
            pl.BlockSpec((d, 2 * C_W), lambda i: (0, 0)),
        ],
        out_specs=[
            pl.BlockSpec((1, n_mem, C_W), lambda i: (i, 0, 0)),
            pl.BlockSpec((1, n_mem, C_W), lambda i: (i, 0, 0)),
        ],
        out_shape=[jax.ShapeDtypeStruct((b, n_mem, C_W), jnp.bfloat16)] * 2,
        name="mem_kv",
    )(mem, g, w)


_IN_WIDTHS = (("qa", A_W), ("ka", A_KV_W), ("va", A_KV_W), ("ga", A_W),
              ("qb", B_W), ("kb", B_W), ("vb", B_W), ("gb", B_W), ("qc", C_W), ("gc", C_W))
D_IN = sum(w for _, w in _IN_WIDTHS)
_DOT_GROUPS = (("qa", "ka"), ("va", "ga"), ("qb", "kb"), ("vb", "gb"), ("qc", "gc"))


def _rope(t, cos_t, sin_t, first8):
    half = ROT_DIM // 2
    out = []
    for j in range(t.shape[1] // LANES):
        tj = t[:, j * LANES:(j + 1) * LANES]
        partner = jnp.where(first8, pltpu.roll(tj, LANES - half, 1), pltpu.roll(tj, half, 1))
        out.append(tj * cos_t + partner * sin_t)
    return out[0] if len(out) == 1 else jnp.concatenate(out, axis=1)


def _expand_kv(t):
    left = _left_half()
    swapped = pltpu.roll(t, HEAD_DIM, 1)
    groups = []
    for p in range(A_Q_HEADS // 2):
        kv_l, kv_r = (2 * p) // A_GROUP, (2 * p + 1) // A_GROUP
        if (kv_l, kv_r) == (0, 1):
            groups.append(t)
        elif kv_l == kv_r == 0:
            groups.append(jnp.where(left, t, swapped))
        else:
            groups.append(jnp.where(left, swapped, t))
    return jnp.concatenate(groups, axis=1)


def _store_dilated(t, refs, stage_ref, stage4_ref):
    rows = t.shape[0]
    slabs = t.shape[1] // LANES
    refs[1][0, 0] = t.astype(refs[1].dtype)
    for sl in range(slabs):
        stage_ref[sl] = t[:, sl * LANES:(sl + 1) * LANES]
    n4, n16 = rows // 4, rows // 16
    for sl in range(slabs):
        cols = slice(sl * LANES, (sl + 1) * LANES)
        for r in range(4):
            c4 = stage_ref[sl, pl.ds(r, n4, stride=4), :]
            refs[4][0, r, :, cols] = c4.astype(refs[4].dtype)
            stage4_ref[sl, r * n4:(r + 1) * n4, :] = c4
        for r in range(4):
            for q in range(4):
                c16 = stage4_ref[sl, pl.ds(r * n4 + q, n16, stride=4), :]
                refs[16][0, r + 4 * q, :, cols] = c16.astype(refs[16].dtype)


def _in_proj_kernel(x_ref, g_ref, w_ref, cos_ref, sin_ref, mk_ref, mv_ref,
                    qa_ref, ka_ref, va_ref, ga_ref, gb_ref, yc_ref, *rest):
    dils = [d for _, d in B_CONFIGS]
    qb_refs = dict(zip(dils, rest[0:3]))
    kb_refs = dict(zip(dils, rest[3:6]))
    vb_refs = dict(zip(dils, rest[6:9]))
    stage_ref, stage4_ref = rest[9:11]
    bf16 = jnp.bfloat16
    xf = x_ref[0]
    u = (xf * _rms_scale(xf) * g_ref[...]).astype(bf16)
    cos_t = cos_ref[...]
    sin_t = sin_ref[...]
    lane = lax.broadcasted_iota(jnp.int32, (1, LANES), 1)
    first8 = (lane % HEAD_DIM) < (ROT_DIM // 2)
    rope = lambda t: _rope(t, cos_t, sin_t, first8)

    offsets = {}
    off = 0
    for name, wd in _IN_WIDTHS:
        offsets[name] = (off, wd)
        off += wd

    def proj(names):
        c0 = offsets[names[0]][0]
        c1 = offsets[names[-1]][0] + offsets[names[-1]][1]
        full = jnp.dot(u, w_ref[:, c0:c1], preferred_element_type=jnp.float32)
        return [full[:, offsets[n][0] - c0:offsets[n][0] - c0 + offsets[n][1]] for n in names]

    qc, gc = proj(_DOT_GROUPS[4])
    qc = (qc * Q_SCALE).astype(bf16)
    ones = jnp.ones((mv_ref.shape[1], LANES), bf16)
    oc = []
    for p in range(C_W // LANES):
        cols = slice(p * LANES, (p + 1) * LANES)
        v_ones = jnp.concatenate([mv_ref[0, :, cols], ones], axis=1)
        o, _ = _pair_attention(qc[:, cols], mk_ref[0, :, cols], v_ones, None)
        oc.append(o)
    yc_ref[0] = (jnp.concatenate(oc, axis=1) * _silu(gc)).astype(bf16)

    qb, kb = proj(_DOT_GROUPS[2])
    _store_dilated(rope(qb) * Q_SCALE, qb_refs, stage_ref, stage4_ref)
    _store_dilated(rope(kb), kb_refs, stage_ref, stage4_ref)
    vb, gb = proj(_DOT_GROUPS[3])
    _store_dilated(vb, vb_refs, stage_ref, stage4_ref)
    gb_ref[0] = _silu(gb).astype(bf16)
    qa, ka = proj(_DOT_GROUPS[0])
    qa_ref[0] = (rope(qa) * Q_SCALE).astype(bf16)
    ka_ref[0] = _expand_kv(rope(ka)).astype(bf16)
    va, ga = proj(_DOT_GROUPS[1])
    va_ref[0] = _expand_kv(va).astype(bf16)
    ga_ref[0] = _silu(ga).astype(bf16)


def _in_proj(x, g, w, cos_t, sin_t, mk, mv):
    b, s, d = x.shape
    tm = IN_ROWS
    n_mem = mk.shape[1]
    row = lambda width: pl.BlockSpec((1, tm, width), lambda i, bi: (bi, i, 0))
    out_widths = (A_W, A_W, A_W, A_W, B_W, C_W)
    out_specs = [row(wd) for wd in out_widths]
    out_shape = [jax.ShapeDtypeStruct((b, s, wd), jnp.bfloat16) for wd in out_widths]
    for _ in range(3):
        for _, dil in B_CONFIGS:
            out_specs.append(pl.BlockSpec((1, dil, tm // dil, B_W), lambda i, bi: (bi, 0, i, 0)))
            out_shape.append(jax.ShapeDtypeStruct((b, dil, s // dil, B_W), jnp.bfloat16))
    stage = pltpu.VMEM((B_W // LANES, tm, LANES), jnp.float32)
    return pl.pallas_call(
        _in_proj_kernel,
        grid=(s // tm, b),
        in_specs=[
            row(d),
            pl.BlockSpec((1, d), lambda i, bi: (0, 0)),
            pl.BlockSpec((d, D_IN), lambda i, bi: (0, 0)),
            pl.BlockSpec((tm, LANES), lambda i, bi: (i, 0)),
            pl.BlockSpec((tm, LANES), lambda i, bi: (i, 0)),
            pl.BlockSpec((1, n_mem, C_W), lambda i, bi: (bi, 0, 0)),
            pl.BlockSpec((1, n_mem, C_W), lambda i, bi: (bi, 0, 0)),
        ],
        out_specs=out_specs,
        out_shape=out_shape,
        scratch_shapes=[stage, stage],
        compiler_params=pltpu.CompilerParams(
            dimension_semantics=("parallel", "parallel"), vmem_limit_bytes=VMEM_LIMIT),
        name="in_proj",
    )(x, g, w, cos_t, sin_t, mk, mv)


def _band_biases(max_dist, first_step, sink_ref, groups):
    qi = lax.broadcasted_iota(jnp.int32, (BLOCK, 2 * BLOCK), 0)
    kc = lax.broadcasted_iota(jnp.int32, (BLOCK, 2 * BLOCK), 1)
    dist = qi + BLOCK - kc
    band = (dist >= 0) & (dist <= max_dist)
    first_key = jnp.where(first_step, BLOCK, 0)
    neg = jnp.float32(-jnp.inf)
    bias_any = jnp.where(band, 0.0, neg)
    bias_first = jnp.where(band & (kc >= first_key), 0.0, neg)

    def pair_bias(base, p):
        if sink_ref is None:
            return jnp.concatenate([base, base], axis=0)
        assert max_dist < BLOCK
        halves = [jnp.where(kc == 0, sink_ref[2 * p + h] * LOG2E, base) for h in range(2)]
        return jnp.concatenate(halves, axis=0)

    if sink_ref is None:
        first, rest = pair_bias(bias_first, 0), pair_bias(bias_any, 0)
        return [first] * groups, [rest] * groups
    return ([pair_bias(bias_first, p) for p in range(groups)],
            [pair_bias(bias_any, p) for p in range(groups)])


def _band_chains(q_ref, kp_ref, ko_ref, vp_ref, vo_ref, bias_0, bias_rest, zero_row0, emit,
                 blocks=None):
    rows = q_ref.shape[2]
    groups = q_ref.shape[3] // LANES
    ones = jnp.ones((2 * BLOCK, LANES), jnp.bfloat16)
    tile = 16
    row0 = lax.broadcasted_iota(jnp.int32, (tile, 1), 0) == 0
    zero_first = lambda t: jnp.concatenate(
        [jnp.where(row0, jnp.zeros_like(t[:tile]), t[:tile]), t[tile:]], axis=0)
    for c in range(q_ref.shape[1]):
        for i in (range(rows // BLOCK) if blocks is None else blocks):
            r = slice(i * BLOCK, (i + 1) * BLOCK)
            q = q_ref[0, c, r, :]
            if i == 0:
                k_prev, v_prev = kp_ref[0, c], vp_ref[0, c]
            else:
                before = slice((i - 1) * BLOCK, i * BLOCK)
                k_prev, v_prev = ko_ref[0, c, before, :], vo_ref[0, c, before, :]
            if zero_row0:
                k_prev, v_prev = zero_first(k_prev), zero_first(v_prev)
            k = jnp.concatenate([k_prev, ko_ref[0, c, r, :]], axis=0)
            v = jnp.concatenate([v_prev, vo_ref[0, c, r, :]], axis=0)
            for p in range(groups):
                cols = slice(p * LANES, (p + 1) * LANES)
                v_ones = jnp.concatenate([v[:, cols], ones], axis=1)
                o, lse2 = _pair_attention(q[:, cols], k[:, cols], v_ones,
                                          bias_0[p] if i == 0 else bias_rest[p])
                emit(c, i, p, o, lse2)


def _qkv_specs(dil, rows, width):
    n = rows // dil
    own = pl.BlockSpec((1, dil, n, width), lambda bi, j: (bi, 0, j, 0))
    prev = pl.BlockSpec(
        (1, dil, BLOCK, width), lambda bi, j: (bi, 0, jnp.maximum(j * (n // BLOCK) - 1, 0), 0))
    return [own, prev, own, prev, own]


def _attn_a_kernel(sink_ref, q_ref, kp_ref, ko_ref, vp_ref, vo_ref, gate_ref, y_ref):
    bias_0, bias_rest = _band_biases(A_WINDOW - 1, pl.program_id(1) == 0, sink_ref,
                                     q_ref.shape[3] // LANES)

    def emit(c, i, p, o, lse2):
        r = slice(i * BLOCK, (i + 1) * BLOCK)
        cols = slice(p * LANES, (p + 1) * LANES)
        y_ref[0, r, cols] = (o * gate_ref[0, r, cols].astype(jnp.float32)).astype(y_ref.dtype)

    _band_chains(q_ref, kp_ref, ko_ref, vp_ref, vo_ref, bias_0, bias_rest, True, emit)


def _attn_a(q, k, v, gate, sink):
    b, _, s, wq = q.shape
    rows = ATTN_ROWS
    row = pl.BlockSpec((1, rows, wq), lambda bi, j: (bi, j, 0))
    return pl.pallas_call(
        _attn_a_kernel,
        grid=(b, s // rows),
        in_specs=[pl.BlockSpec(memory_space=pltpu.SMEM)] + _qkv_specs(1, rows, wq) + [row],
        out_specs=row,
        out_shape=jax.ShapeDtypeStruct((b, s, wq), jnp.bfloat16),
        compiler_params=pltpu.CompilerParams(
            dimension_semantics=("parallel", "parallel"), vmem_limit_bytes=VMEM_LIMIT),
        name="attn_a",
    )(sink, q, k, k, v, v, gate)


def _attn_b_kernel(*refs):
    n_cfg = len(B_CONFIGS)
    gate_ref, y_ref = refs[5 * n_cfg:5 * n_cfg + 2]
    stage = list(refs[5 * n_cfg + 2:])
    span = y_ref.shape[1]
    first_step = pl.program_id(1) == 0
    groups = B_W // LANES
    order = sorted(range(n_cfg), key=lambda c: -B_CONFIGS[c][1])
    assert B_CONFIGS[order[-1]][1] == 1
    qkv, biases, cover = {}, {}, {}
    for c in order:
        win, dil = B_CONFIGS[c]
        qkv[dil] = refs[5 * c:5 * c + 5]
        biases[dil] = _band_biases(win // dil, first_step, None, groups)
        cover[dil] = dil * BLOCK
    dils = [B_CONFIGS[c][1] for c in order]
    staged = {dil: [(stage.pop(0), stage.pop(0)) for _ in range(span // cover[dil])]
              for dil in dils[:-1]}

    def run(level, lo):
        dil = dils[level]
        blk = lo // cover[dil]

        def emit_staged(cls, i, p, o, lse2):
            o_stage, l_stage = staged[dil][blk]
            rows = pl.ds(cls, BLOCK, stride=dil)
            o_stage[p, rows, :] = o
            l_stage[p, rows, :] = lse2

        def emit_mixed(cls, i, p, o, lse2):
            cols = slice(p * LANES, (p + 1) * LANES)
            outs, lses = [o], [lse2]
            for coarse in dils[:-1]:
                o_stage, l_stage = staged[coarse][lo // cover[coarse]]
                r = slice(lo % cover[coarse], lo % cover[coarse] + BLOCK)
                outs.append(o_stage[p, r, :])
                lses.append(l_stage[p, r, :])
            mx = functools.reduce(jnp.maximum, lses)
            wts = [jnp.exp2(l - mx) for l in lses]
            num = sum(w * ov for w, ov in zip(wts[1:], outs[1:])) + wts[0] * outs[0]
            ob = num * (1.0 / sum(wts[1:], wts[0]))
            r = slice(lo, lo + BLOCK)
            gate = gate_ref[0, r, cols].astype(jnp.float32)
            y_ref[0, r, cols] = (ob * gate).astype(y_ref.dtype)

        last = level == len(dils) - 1
        _band_chains(*qkv[dil], *biases[dil], False, emit_mixed if last else emit_staged,
                     blocks=(blk,))
        if not last:
            for sub in range(lo, lo + cover[dil], cover[dils[level + 1]]):
                run(level + 1, sub)

    for lo in range(0, span, cover[dils[0]]):
        run(0, lo)


def _attn_b(qkv_b, gate):
    n_cfg = len(B_CONFIGS)
    b, _, s, wq = qkv_b[0].shape
    rows = ATTN_ROWS
    in_specs, args = [], []
    for c, (_, dil) in enumerate(B_CONFIGS):
        q, k, v = qkv_b[c], qkv_b[n_cfg + c], qkv_b[2 * n_cfg + c]
        in_specs += _qkv_specs(dil, rows, wq)
        args += [q, k, k, v, v]
    row = pl.BlockSpec((1, rows, wq), lambda bi, j: (bi, j, 0))
    stages = []
    for dil in sorted((d for _, d in B_CONFIGS if d > 1), reverse=True):
        cover = dil * BLOCK
        stages += [pltpu.VMEM((wq // LANES, cover, LANES), jnp.float32)] * (2 * (rows // cover))
    return pl.pallas_call(
        _attn_b_kernel,
        grid=(b, s // rows),
        in_specs=in_specs + [row],
        out_specs=row,
        out_shape=jax.ShapeDtypeStruct((b, s, wq), jnp.bfloat16),
        scratch_shapes=stages,
        compiler_params=pltpu.CompilerParams(
            dimension_semantics=("parallel", "parallel"), vmem_limit_bytes=ATTN_B_VMEM_LIMIT),
        name="attn_b",
    )(*args, gate)


def _out_proj_kernel(x_ref, ya_ref, yb_ref, yc_ref, w_ref, g_ref, out_ref):
    y = jnp.concatenate([ya_ref[0], yb_ref[0], yc_ref[0]], axis=1)
    z = jnp.dot(y, w_ref[...], preferred_element_type=jnp.float32)
    out_ref[0] = x_ref[0] + z * _rms_scale(z) * g_ref[...]


def _out_proj(x, ya, yb, yc, w, g):
    b, s, d = x.shape
    tm = PROJ_ROWS
    row = lambda width: pl.BlockSpec((1, tm, width), lambda bi, i: (bi, i, 0))
    return pl.pallas_call(
        _out_proj_kernel,
        grid=(b, s // tm),
        in_specs=[row(d), row(A_W), row(B_W), row(C_W),
                  pl.BlockSpec((d, d), lambda bi, i: (0, 0)),
                  pl.BlockSpec((1, d), lambda bi, i: (0, 0))],
        out_specs=row(d),
        out_shape=jax.ShapeDtypeStruct((b, s, d), x.dtype),
        compiler_params=pltpu.CompilerParams(
            dimension_semantics=("parallel", "parallel"), vmem_limit_bytes=VMEM_LIMIT),
        name="out_proj",
    )(x, ya, yb, yc, w, g)


def _rope_tables(seq):
    inv_freq = ROPE_THETA ** (-jnp.arange(0, ROT_DIM, 2, dtype=jnp.float32) / ROT_DIM)
    ang = jnp.arange(seq, dtype=jnp.float32)[:, None] * inv_freq[None, :]
    cos, sin = jnp.cos(ang), jnp.sin(ang)
    pad = HEAD_DIM - ROT_DIM
    cos_h = jnp.concatenate([cos, cos, jnp.ones((seq, pad), jnp.float32)], axis=1)
    sin_h = jnp.concatenate([-sin, sin, jnp.zeros((seq, pad), jnp.float32)], axis=1)
    reps = LANES // HEAD_DIM
    return jnp.tile(cos_h, (1, reps)), jnp.tile(sin_h, (1, reps))


def kernel(x, mem, pre_norm, w_in, sink_a, mem_norm, w_mem_kv, w_out, post_norm):
    depth = pre_norm.shape[0]
    cos_t, sin_t = _rope_tables(x.shape[1])
    bf16 = jnp.bfloat16
    h = x
    for l in range(depth):
        mk, mv = _mem_kv(mem, mem_norm[l][None], w_mem_kv[l].astype(bf16))
        qa, ka, va, ga, gb, yc, *qkv_b = _in_proj(
            h, pre_norm[l][None], w_in[l].astype(bf16), cos_t, sin_t, mk, mv)
        ya = _attn_a(qa[:, None], ka[:, None], va[:, None], ga, sink_a[l])
        yb = _attn_b(qkv_b, gb)
        h = _out_proj(h, ya, yb, yc, w_out[l].astype(bf16), post_norm[l][None])
    return h
```

```python
import functools
import math

import jax
import jax.numpy as jnp
import numpy as np
from jax import lax
from jax.experimental import pallas as pl
from jax.experimental.pallas import tpu as pltpu

HEAD_DIM = 64
ROT_DIM = HEAD_DIM // 4
ROPE_THETA = 500000.0
BLOCK = 128
C_HEADS = 4
A_Q_HEADS = 6
A_KV_HEADS = 2
A_GROUP = A_Q_HEADS // A_KV_HEADS
A_WINDOW = 128
B_HEADS = 6
B_CONFIGS = ((128, 1), (512, 4), (2048, 16))
RMS_EPS = 1e-6

LANES = 128
A_W = A_Q_HEADS * HEAD_DIM
A_KV_W = A_KV_HEADS * HEAD_DIM
B_W = B_HEADS * HEAD_DIM
C_W = C_HEADS * HEAD_DIM

IN_ROWS = 1024
ATTN_ROWS = 2048
OUT_ROWS = 1024
OUT_GROUP = 1024
VMEM_LIMIT = 48 * 1024 * 1024
ATTN_B_VMEM_LIMIT = 56 * 1024 * 1024

LOG2E = math.log2(math.e)
Q_SCALE = HEAD_DIM ** -0.5 * LOG2E


def _rms_scale(xf):
    return lax.rsqrt(jnp.mean(xf * xf, axis=-1, keepdims=True) + RMS_EPS)


def _silu(g):
    return g * (1.0 / (1.0 + jnp.exp(-g)))


def _left_half():
    return lax.broadcasted_iota(jnp.int32, (1, LANES), 1) < HEAD_DIM


def _pair_attention(q2, kg, v_ones, bias):
    rows = q2.shape[0]
    left = _left_half()
    keep_l = jnp.where(left, 1.0, 0.0).astype(q2.dtype)
    qs = jnp.concatenate([q2 * keep_l, q2 * (1 - keep_l)], axis=0)
    s = lax.dot_general(qs, kg, (((1,), (1,)), ((), ())), preferred_element_type=jnp.float32)
    if bias is not None:
        s = s + bias
    m = jnp.max(s, axis=-1, keepdims=True)
    e = jnp.exp2((s - m).astype(jnp.bfloat16))
    pv = jnp.dot(e, v_ones, preferred_element_type=jnp.float32)
    acc = jnp.where(left, pv[:rows, :LANES], pv[rows:, :LANES])
    l = jnp.where(left, pv[:rows, LANES:], pv[rows:, LANES:])
    m_own = jnp.where(left, m[:rows], m[rows:])
    return acc * (1.0 / l), m_own + jnp.log(l) * LOG2E


def _mem_kv_kernel(mem_ref, g_ref, w_ref, mk_ref, mv_ref):
    m = mem_ref[0]
    u = (m * _rms_scale(m) * g_ref[...]).astype(jnp.bfloat16)
    kv = jnp.dot(u, w_ref[...], preferred_element_type=jnp.float32)
    mk_ref[0] = kv[:, :C_W].astype(jnp.bfloat16)
    mv_ref[0] = kv[:, C_W:].astype(jnp.bfloat16)


def _mem_kv(mem, g, w):
    b, n_mem, d = mem.shape
    return pl.pallas_call(
        _mem_kv_kernel,
        grid=(b,),
        in_specs=[
            pl.BlockSpec((1, n_mem, d), lambda i: (i, 0, 0)),
            pl.BlockSpec((1, d), lambda i: (0, 0)),
            pl.BlockSpec((d, 2 * C_W), lambda i: (0, 0)),
        ],
        out_specs=[
            pl.BlockSpec((1, n_mem, C_W), lambda i: (i, 0, 0)),
            pl.BlockSpec((1, n_mem, C_W), lambda i: (i, 0, 0)),
        ],
        out_shape=[jax.ShapeDtypeStruct((b, n_mem, C_W), jnp.bfloat16)] * 2,
        name="mem_kv",
    )(mem, g, w)


_IN_WIDTHS = (("qa", A_W), ("ka", A_KV_W), ("va", A_KV_W), ("ga", A_W),
              ("qb", B_W), ("kb", B_W), ("vb", B_W), ("gb", B_W), ("qc", C_W), ("gc", C_W))
D_IN = sum(w for _, w in _IN_WIDTHS)
_DOT_GROUPS = (("qa", "ka"), ("va", "ga"), ("qb", "kb"), ("vb", "gb"), ("qc", "gc"))


def _rope(t, cos_t, sin_t, first8):
    half = ROT_DIM // 2
    out = []
    for j in range(t.shape[1] // LANES):
        tj = t[:, j * LANES:(j + 1) * LANES]
        partner = jnp.where(first8, pltpu.roll(tj, LANES - half, 1), pltpu.roll(tj, half, 1))
        out.append(tj * cos_t + partner * sin_t)
    return out[0] if len(out) == 1 else jnp.concatenate(out, axis=1)


def _expand_kv(t):
    left = _left_half()
    swapped = pltpu.roll(t, HEAD_DIM, 1)
    groups = []
    for p in range(A_Q_HEADS // 2):
        kv_l, kv_r = (2 * p) // A_GROUP, (2 * p + 1) // A_GROUP
        if (kv_l, kv_r) == (0, 1):
            groups.append(t)
        elif kv_l == kv_r == 0:
            groups.append(jnp.where(left, t, swapped))
        else:
            groups.append(jnp.where(left, swapped, t))
    return jnp.concatenate(groups, axis=1)


def _store_dilated(t, refs, stage_ref, stage4_ref):
    rows = t.shape[0]
    slabs = t.shape[1] // LANES
    refs[1][0, 0] = t.astype(refs[1].dtype)
    for sl in range(slabs):
        stage_ref[sl] = t[:, sl * LANES:(sl + 1) * LANES]
    n4, n16 = rows // 4, rows // 16
    for sl in range(slabs):
        cols = slice(sl * LANES, (sl + 1) * LANES)
        for r in range(4):
            c4 = stage_ref[sl, pl.ds(r, n4, stride=4), :]
            refs[4][0, r, :, cols] = c4.astype(refs[4].dtype)
            stage4_ref[sl, r * n4:(r + 1) * n4, :] = c4
        for r in range(4):
            for q in range(4):
                c16 = stage4_ref[sl, pl.ds(r * n4 + q, n16, stride=4), :]
                refs[16][0, r + 4 * q, :, cols] = c16.astype(refs[16].dtype)


def _in_proj_kernel(x_ref, g_ref, w_ref, cos_ref, sin_ref, mk_ref, mv_ref,
                    qa_ref, ka_ref, va_ref, ga_ref, gb_ref, yc_ref, *rest):
    dils = [d for _, d in B_CONFIGS]
    qb_refs = dict(zip(dils, rest[0:3]))
    kb_refs = dict(zip(dils, rest[3:6]))
    vb_refs = dict(zip(dils, rest[6:9]))
    stage_ref, stage4_ref = rest[9:11]
    bf16 = jnp.bfloat16
    xf = x_ref[0]
    u = (xf * _rms_scale(xf) * g_ref[...]).astype(bf16)
    cos_t = cos_ref[...]
    sin_t = sin_ref[...]
    lane = lax.broadcasted_iota(jnp.int32, (1, LANES), 1)
    first8 = (lane % HEAD_DIM) < (ROT_DIM // 2)
    rope = lambda t: _rope(t, cos_t, sin_t, first8)

    offsets = {}
    off = 0
    for name, wd in _IN_WIDTHS:
        offsets[name] = (off, wd)
        off += wd

    def proj(names):
        c0 = offsets[names[0]][0]
        c1 = offsets[names[-1]][0] + offsets[names[-1]][1]
        full = jnp.dot(u, w_ref[:, c0:c1], preferred_element_type=jnp.float32)
        return [full[:, offsets[n][0] - c0:offsets[n][0] - c0 + offsets[n][1]] for n in names]

    qc, gc = proj(_DOT_GROUPS[4])
    qc = (qc * Q_SCALE).astype(bf16)
    ones = jnp.ones((mv_ref.shape[1], LANES), bf16)
    oc = []
    for p in range(C_W // LANES):
        cols = slice(p * LANES, (p + 1) * LANES)
        v_ones = jnp.concatenate([mv_ref[0, :, cols], ones], axis=1)
        o, _ = _pair_attention(qc[:, cols], mk_ref[0, :, cols], v_ones, None)
        oc.append(o)
    yc_ref[0] = (jnp.concatenate(oc, axis=1) * _silu(gc)).astype(bf16)

    qb, kb = proj(_DOT_GROUPS[2])
    _store_dilated(rope(qb) * Q_SCALE, qb_refs, stage_ref, stage4_ref)
    _store_dilated(rope(kb), kb_refs, stage_ref, stage4_ref)
    vb, gb = proj(_DOT_GROUPS[3])
    _store_dilated(vb, vb_refs, stage_ref, stage4_ref)
    gb_ref[0] = _silu(gb).astype(bf16)
    qa, ka = proj(_DOT_GROUPS[0])
    qa_ref[0] = (rope(qa) * Q_SCALE).astype(bf16)
    ka_ref[0] = _expand_kv(rope(ka)).astype(bf16)
    va, ga = proj(_DOT_GROUPS[1])
    va_ref[0] = _expand_kv(va).astype(bf16)
    ga_ref[0] = _silu(ga).astype(bf16)


def _in_proj(x, g, w, cos_t, sin_t, mk, mv):
    b, s, d = x.shape
    tm = IN_ROWS
    n_mem = mk.shape[1]
    row = lambda width: pl.BlockSpec((1, tm, width), lambda i, bi: (bi, i, 0))
    out_widths = (A_W, A_W, A_W, A_W, B_W, C_W)
    out_specs = [row(wd) for wd in out_widths]
    out_shape = [jax.ShapeDtypeStruct((b, s, wd), jnp.bfloat16) for wd in out_widths]
    for _ in range(3):
        for _, dil in B_CONFIGS:
            out_specs.append(pl.BlockSpec((1, dil, tm // dil, B_W), lambda i, bi: (bi, 0, i, 0)))
            out_shape.append(jax.ShapeDtypeStruct((b, dil, s // dil, B_W), jnp.bfloat16))
    stage = pltpu.VMEM((B_W // LANES, tm, LANES), jnp.float32)
    return pl.pallas_call(
        _in_proj_kernel,
        grid=(s // tm, b),
        in_specs=[
            row(d),
            pl.BlockSpec((1, d), lambda i, bi: (0, 0)),
            pl.BlockSpec((d, D_IN), lambda i, bi: (0, 0)),
            pl.BlockSpec((tm, LANES), lambda i, bi: (i, 0)),
            pl.BlockSpec((tm, LANES), lambda i, bi: (i, 0)),
            pl.BlockSpec((1, n_mem, C_W), lambda i, bi: (bi, 0, 0)),
            pl.BlockSpec((1, n_mem, C_W), lambda i, bi: (bi, 0, 0)),
        ],
        out_specs=out_specs,
        out_shape=out_shape,
        scratch_shapes=[stage, stage],
        compiler_params=pltpu.CompilerParams(
            dimension_semantics=("parallel", "parallel"), vmem_limit_bytes=VMEM_LIMIT),
        name="in_proj",
    )(x, g, w, cos_t, sin_t, mk, mv)


def _band_biases(max_dist, first_step, sink_ref, groups):
    qi = lax.broadcasted_iota(jnp.int32, (BLOCK, 2 * BLOCK), 0)
    kc = lax.broadcasted_iota(jnp.int32, (BLOCK, 2 * BLOCK), 1)
    dist = qi + BLOCK - kc
    band = (dist >= 0) & (dist <= max_dist)
    first_key = jnp.where(first_step, BLOCK, 0)
    neg = jnp.float32(-jnp.inf)
    bias_any = jnp.where(band, 0.0, neg)
    bias_first = jnp.where(band & (kc >= first_key), 0.0, neg)

    def pair_bias(base, p):
        if sink_ref is None:
            return jnp.concatenate([base, base], axis=0)
        assert max_dist < BLOCK
        halves = [jnp.where(kc == 0, sink_ref[2 * p + h] * LOG2E, base) for h in range(2)]
        return jnp.concatenate(halves, axis=0)

    if sink_ref is None:
        first, rest = pair_bias(bias_first, 0), pair_bias(bias_any, 0)
        return [first] * groups, [rest] * groups
    return ([pair_bias(bias_first, p) for p in range(groups)],
            [pair_bias(bias_any, p) for p in range(groups)])


def _band_chains(q_ref, kp_ref, ko_ref, vp_ref, vo_ref, bias_0, bias_rest, zero_row0, emit,
                 blocks=None):
    rows = q_ref.shape[2]
    groups = q_ref.shape[3] // LANES
    ones = jnp.ones((2 * BLOCK, LANES), jnp.bfloat16)
    tile = 16
    row0 = lax.broadcasted_iota(jnp.int32, (tile, 1), 0) == 0
    zero_first = lambda t: jnp.concatenate(
        [jnp.where(row0, jnp.zeros_like(t[:tile]), t[:tile]), t[tile:]], axis=0)
    for c in range(q_ref.shape[1]):
        for i in (range(rows // BLOCK) if blocks is None else blocks):
            r = slice(i * BLOCK, (i + 1) * BLOCK)
            q = q_ref[0, c, r, :]
            if i == 0:
                k_prev, v_prev = kp_ref[0, c], vp_ref[0, c]
            else:
                before = slice((i - 1) * BLOCK, i * BLOCK)
                k_prev, v_prev = ko_ref[0, c, before, :], vo_ref[0, c, before, :]
            if zero_row0:
                k_prev, v_prev = zero_first(k_prev), zero_first(v_prev)
            k = jnp.concatenate([k_prev, ko_ref[0, c, r, :]], axis=0)
            v = jnp.concatenate([v_prev, vo_ref[0, c, r, :]], axis=0)
            for p in range(groups):
                cols = slice(p * LANES, (p + 1) * LANES)
                v_ones = jnp.concatenate([v[:, cols], ones], axis=1)
                o, lse2 = _pair_attention(q[:, cols], k[:, cols], v_ones,
                                          bias_0[p] if i == 0 else bias_rest[p])
                emit(c, i, p, o, lse2)


def _qkv_specs(dil, rows, width):
    n = rows // dil
    own = pl.BlockSpec((1, dil, n, width), lambda bi, j: (bi, 0, j, 0))
    prev = pl.BlockSpec(
        (1, dil, BLOCK, width), lambda bi, j: (bi, 0, jnp.maximum(j * (n // BLOCK) - 1, 0), 0))
    return [own, prev, own, prev, own]


def _attn_a_out_kernel(sink_ref, q_ref, kp_ref, ko_ref, vp_ref, vo_ref, gate_ref,
                       yb_ref, yc_ref, x_ref, w_ref, g_ref, out_ref):
    groups = q_ref.shape[3] // LANES
    bias_0, bias_rest = _band_biases(A_WINDOW - 1, pl.program_id(1) == 0, sink_ref, groups)
    tiles = {}

    def emit(c, i, p, o, lse2):
        r = slice(i * BLOCK, (i + 1) * BLOCK)
        cols = slice(p * LANES, (p + 1) * LANES)
        tiles[i, p] = (o * gate_ref[0, r, cols].astype(jnp.float32)).astype(jnp.bfloat16)

    per_group = OUT_GROUP // BLOCK
    for grp in range(q_ref.shape[2] // OUT_GROUP):
        blocks = range(grp * per_group, (grp + 1) * per_group)
        _band_chains(q_ref, kp_ref, ko_ref, vp_ref, vo_ref, bias_0, bias_rest, True, emit,
                     blocks=blocks)
        r = slice(grp * OUT_GROUP, (grp + 1) * OUT_GROUP)
        ya = jnp.concatenate(
            [jnp.concatenate([tiles[i, p] for p in range(groups)], axis=1) for i in blocks], axis=0)
        y = jnp.concatenate([ya, yb_ref[0, r, :], yc_ref[0, r, :]], axis=1)
        z = jnp.dot(y, w_ref[...], preferred_element_type=jnp.float32)
        out_ref[0, r, :] = x_ref[0, r, :] + z * _rms_scale(z) * g_ref[...]


def _attn_a_out(q, k, v, gate, sink, yb, yc, x, w, g):
    b, _, s, wq = q.shape
    d = x.shape[2]
    rows = OUT_ROWS
    row = lambda width: pl.BlockSpec((1, rows, width), lambda bi, j: (bi, j, 0))
    return pl.pallas_call(
        _attn_a_out_kernel,
        grid=(b, s // rows),
        in_specs=[pl.BlockSpec(memory_space=pltpu.SMEM)] + _qkv_specs(1, rows, wq)
        + [row(wq), row(yb.shape[2]), row(yc.shape[2]), row(d),
           pl.BlockSpec((d, d), lambda bi, j: (0, 0)),
           pl.BlockSpec((1, d), lambda bi, j: (0, 0))],
        out_specs=row(d),
        out_shape=jax.ShapeDtypeStruct((b, s, d), x.dtype),
        compiler_params=pltpu.CompilerParams(
            dimension_semantics=("parallel", "parallel"), vmem_limit_bytes=VMEM_LIMIT),
        name="attn_a_out",
    )(sink, q, k, k, v, v, gate, yb, yc, x, w, g)


def _attn_b_kernel(*refs):
    n_cfg = len(B_CONFIGS)
    gate_ref, y_ref = refs[5 * n_cfg:5 * n_cfg + 2]
    stage = list(refs[5 * n_cfg + 2:])
    span = y_ref.shape[1]
    first_step = pl.program_id(1) == 0
    groups = B_W // LANES
    order = sorted(range(n_cfg), key=lambda c: -B_CONFIGS[c][1])
    assert B_CONFIGS[order[-1]][1] == 1
    qkv, biases, cover = {}, {}, {}
    for c in order:
        win, dil = B_CONFIGS[c]
        qkv[dil] = refs[5 * c:5 * c + 5]
        biases[dil] = _band_biases(win // dil, first_step, None, groups)
        cover[dil] = dil * BLOCK
    dils = [B_CONFIGS[c][1] for c in order]
    staged = {dil: [(stage.pop(0), stage.pop(0)) for _ in range(span // cover[dil])]
              for dil in dils[:-1]}

    def run(level, lo):
        dil = dils[level]
        blk = lo // cover[dil]

        def emit_staged(cls, i, p, o, lse2):
            o_stage, l_stage = staged[dil][blk]
            rows = pl.ds(cls, BLOCK, stride=dil)
            o_stage[p, rows, :] = o
            l_stage[p, rows, :] = lse2

        def emit_mixed(cls, i, p, o, lse2):
            cols = slice(p * LANES, (p + 1) * LANES)
            outs, lses = [o], [lse2]
            for coarse in dils[:-1]:
                o_stage, l_stage = staged[coarse][lo // cover[coarse]]
                r = slice(lo % cover[coarse], lo % cover[coarse] + BLOCK)
                outs.append(o_stage[p, r, :])
                lses.append(l_stage[p, r, :])
            mx = functools.reduce(jnp.maximum, lses)
            wts = [jnp.exp2(l - mx) for l in lses]
            num = sum(w * ov for w, ov in zip(wts[1:], outs[1:])) + wts[0] * outs[0]
            ob = num * (1.0 / sum(wts[1:], wts[0]))
            r = slice(lo, lo + BLOCK)
            gate = gate_ref[0, r, cols].astype(jnp.float32)
            y_ref[0, r, cols] = (ob * gate).astype(y_ref.dtype)

        last = level == len(dils) - 1
        _band_chains(*qkv[dil], *biases[dil], False, emit_mixed if last else emit_staged,
                     blocks=(blk,))
        if not last:
            for sub in range(lo, lo + cover[dil], cover[dils[level + 1]]):
                run(level + 1, sub)

    for lo in range(0, span, cover[dils[0]]):
        run(0, lo)


def _attn_b(qkv_b, gate):
    n_cfg = len(B_CONFIGS)
    b, _, s, wq = qkv_b[0].shape
    rows = ATTN_ROWS
    in_specs, args = [], []
    for c, (_, dil) in enumerate(B_CONFIGS):
        q, k, v = qkv_b[c], qkv_b[n_cfg + c], qkv_b[2 * n_cfg + c]
        in_specs += _qkv_specs(dil, rows, wq)
        args += [q, k, k, v, v]
    row = pl.BlockSpec((1, rows, wq), lambda bi, j: (bi, j, 0))
    stages = []
    for dil in sorted((d for _, d in B_CONFIGS if d > 1), reverse=True):
        cover = dil * BLOCK
        stages += [pltpu.VMEM((wq // LANES, cover, LANES), jnp.float32)] * (2 * (rows // cover))
    return pl.pallas_call(
        _attn_b_kernel,
        grid=(b, s // rows),
        in_specs=in_specs + [row],
        out_specs=row,
        out_shape=jax.ShapeDtypeStruct((b, s, wq), jnp.bfloat16),
        scratch_shapes=stages,
        compiler_params=pltpu.CompilerParams(
            dimension_semantics=("parallel", "parallel"), vmem_limit_bytes=ATTN_B_VMEM_LIMIT),
        name="attn_b",
    )(*args, gate)


def _rope_tables(seq):
    inv_freq = ROPE_THETA ** (-np.arange(0, ROT_DIM, 2, dtype=np.float64) / ROT_DIM)
    ang = np.arange(seq, dtype=np.float64)[:, None] * inv_freq[None, :]
    cos, sin = np.cos(ang), np.sin(ang)
    pad = HEAD_DIM - ROT_DIM
    cos_h = np.concatenate([cos, cos, np.ones((seq, pad))], axis=1)
    sin_h = np.concatenate([-sin, sin, np.zeros((seq, pad))], axis=1)
    reps = LANES // HEAD_DIM
    return (jnp.asarray(np.tile(cos_h, (1, reps)), jnp.float32),
            jnp.asarray(np.tile(sin_h, (1, reps)), jnp.float32))


def kernel(x, mem, pre_norm, w_in, sink_a, mem_norm, w_mem_kv, w_out, post_norm):
    depth = pre_norm.shape[0]
    cos_t, sin_t = _rope_tables(x.shape[1])
    bf16 = jnp.bfloat16
    h = x
    for l in range(depth):
        mk, mv = _mem_kv(mem, mem_norm[l][None], w_mem_kv[l].astype(bf16))
        qa, ka, va, ga, gb, yc, *qkv_b = _in_proj(
            h, pre_norm[l][None], w_in[l].astype(bf16), cos_t, sin_t, mk, mv)
        yb = _attn_b(qkv_b, gb)
        h = _attn_a_out(qa[:, None], ka[:, None], va[:, None], ga, sink_a[l], yb, yc, h,
                        w_out[l].astype(bf16), post_norm[l][None])
    return h
```

```python
import functools
import math

import jax
import jax.numpy as jnp
import numpy as np
from jax import lax
from jax.experimental import pallas as pl
from jax.experimental.pallas import tpu as pltpu

HEAD_DIM = 64
ROT_DIM = HEAD_DIM // 4
ROPE_THETA = 500000.0
BLOCK = 128
C_HEADS = 4
A_Q_HEADS = 6
A_KV_HEADS = 2
A_GROUP = A_Q_HEADS // A_KV_HEADS
A_WINDOW = 128
B_HEADS = 6
B_CONFIGS = ((128, 1), (512, 4), (2048, 16))
RMS_EPS = 1e-6

LANES = 128
A_W = A_Q_HEADS * HEAD_DIM
A_KV_W = A_KV_HEADS * HEAD_DIM
B_W = B_HEADS * HEAD_DIM
C_W = C_HEADS * HEAD_DIM

IN_ROWS = 1024
ATTN_ROWS = 2048
OUT_ROWS = 1024
OUT_GROUP = 1024
VMEM_LIMIT = 48 * 1024 * 1024
ATTN_B_VMEM_LIMIT = 56 * 1024 * 1024

LOG2E = math.log2(math.e)
Q_SCALE = HEAD_DIM ** -0.5 * LOG2E


def _rms_scale(xf):
    return lax.rsqrt(jnp.mean(xf * xf, axis=-1, keepdims=True) + RMS_EPS)


def _silu(g):
    return g * (1.0 / (1.0 + jnp.exp(-g)))


def _left_half():
    return lax.broadcasted_iota(jnp.int32, (1, LANES), 1) < HEAD_DIM


def _pair_attention(q2, kg, v_ones, bias):
    rows = q2.shape[0]
    left = _left_half()
    keep_l = jnp.where(left, 1.0, 0.0).astype(q2.dtype)
    qs = jnp.concatenate([q2 * keep_l, q2 * (1 - keep_l)], axis=0)
    s = lax.dot_general(qs, kg, (((1,), (1,)), ((), ())), preferred_element_type=jnp.float32)
    if bias is not None:
        s = s + bias
    m = jnp.max(s, axis=-1, keepdims=True)
    e = jnp.exp2((s - m).astype(jnp.bfloat16))
    pv = jnp.dot(e, v_ones, preferred_element_type=jnp.float32)
    acc = jnp.where(left, pv[:rows, :LANES], pv[rows:, :LANES])
    l = jnp.where(left, pv[:rows, LANES:], pv[rows:, LANES:])
    m_own = jnp.where(left, m[:rows], m[rows:])
    return acc, l, m_own


def _mem_kv_kernel(mem_ref, g_ref, w_ref, mk_ref, mv_ref):
    m = mem_ref[0]
    u = (m * _rms_scale(m) * g_ref[...]).astype(jnp.bfloat16)
    kv = jnp.dot(u, w_ref[...], preferred_element_type=jnp.float32)
    mk_ref[0] = kv[:, :C_W].astype(jnp.bfloat16)
    mv_ref[0] = kv[:, C_W:].astype(jnp.bfloat16)


def _mem_kv(mem, g, w):
    b, n_mem, d = mem.shape
    return pl.pallas_call(
        _mem_kv_kernel,
        grid=(b,),
        in_specs=[
            pl.BlockSpec((1, n_mem, d), lambda i: (i, 0, 0)),
            pl.BlockSpec((1, d), lambda i: (0, 0)),
            pl.BlockSpec((d, 2 * C_W), lambda i: (0, 0)),
        ],
        out_specs=[
            pl.BlockSpec((1, n_mem, C_W), lambda i: (i, 0, 0)),
            pl.BlockSpec((1, n_mem, C_W), lambda i: (i, 0, 0)),
        ],
        out_shape=[jax.ShapeDtypeStruct((b, n_mem, C_W), jnp.bfloat16)] * 2,
        name="mem_kv",
    )(mem, g, w)


_IN_WIDTHS = (("qa", A_W), ("ka", A_KV_W), ("va", A_KV_W), ("ga", A_W),
              ("qb", B_W), ("kb", B_W), ("vb", B_W), ("gb", B_W), ("qc", C_W), ("gc", C_W))
D_IN = sum(w for _, w in _IN_WIDTHS)
_DOT_GROUPS = (("qa", "ka"), ("va", "ga"), ("qb", "kb"), ("vb", "gb"), ("qc", "gc"))


def _rope(t, cos_t, sin_t, first8):
    half = ROT_DIM // 2
    out = []
    for j in range(t.shape[1] // LANES):
        tj = t[:, j * LANES:(j + 1) * LANES]
        partner = jnp.where(first8, pltpu.roll(tj, LANES - half, 1), pltpu.roll(tj, half, 1))
        out.append(tj * cos_t + partner * sin_t)
    return out[0] if len(out) == 1 else jnp.concatenate(out, axis=1)


def _expand_kv(t):
    left = _left_half()
    swapped = pltpu.roll(t, HEAD_DIM, 1)
    groups = []
    for p in range(A_Q_HEADS // 2):
        kv_l, kv_r = (2 * p) // A_GROUP, (2 * p + 1) // A_GROUP
        if (kv_l, kv_r) == (0, 1):
            groups.append(t)
        elif kv_l == kv_r == 0:
            groups.append(jnp.where(left, t, swapped))
        else:
            groups.append(jnp.where(left, swapped, t))
    return jnp.concatenate(groups, axis=1)


def _store_dilated(t, refs, stage_ref, stage4_ref):
    rows = t.shape[0]
    slabs = t.shape[1] // LANES
    refs[1][0, 0] = t.astype(refs[1].dtype)
    for sl in range(slabs):
        stage_ref[sl] = t[:, sl * LANES:(sl + 1) * LANES]
    n4, n16 = rows // 4, rows // 16
    for sl in range(slabs):
        cols = slice(sl * LANES, (sl + 1) * LANES)
        for r in range(4):
            c4 = stage_ref[sl, pl.ds(r, n4, stride=4), :]
            refs[4][0, r, :, cols] = c4.astype(refs[4].dtype)
            stage4_ref[sl, r * n4:(r + 1) * n4, :] = c4
        for r in range(4):
            for q in range(4):
                c16 = stage4_ref[sl, pl.ds(r * n4 + q, n16, stride=4), :]
                refs[16][0, r + 4 * q, :, cols] = c16.astype(refs[16].dtype)


def _in_proj_kernel(x_ref, g_ref, w_ref, cos_ref, sin_ref, mk_ref, mv_ref,
                    qa_ref, ka_ref, va_ref, ga_ref, gb_ref, yc_ref, *rest):
    dils = [d for _, d in B_CONFIGS]
    qb_refs = dict(zip(dils, rest[0:3]))
    kb_refs = dict(zip(dils, rest[3:6]))
    vb_refs = dict(zip(dils, rest[6:9]))
    stage_ref, stage4_ref = rest[9:11]
    bf16 = jnp.bfloat16
    xf = x_ref[0]
    u = (xf * _rms_scale(xf) * g_ref[...]).astype(bf16)
    cos_t = cos_ref[...]
    sin_t = sin_ref[...]
    lane = lax.broadcasted_iota(jnp.int32, (1, LANES), 1)
    first8 = (lane % HEAD_DIM) < (ROT_DIM // 2)
    rope = lambda t: _rope(t, cos_t, sin_t, first8)

    offsets = {}
    off = 0
    for name, wd in _IN_WIDTHS:
        offsets[name] = (off, wd)
        off += wd

    def proj(names):
        c0 = offsets[names[0]][0]
        c1 = offsets[names[-1]][0] + offsets[names[-1]][1]
        full = jnp.dot(u, w_ref[:, c0:c1], preferred_element_type=jnp.float32)
        return [full[:, offsets[n][0] - c0:offsets[n][0] - c0 + offsets[n][1]] for n in names]

    qc, gc = proj(_DOT_GROUPS[4])
    qc = (qc * Q_SCALE).astype(bf16)
    ones = jnp.ones((mv_ref.shape[1], LANES), bf16)
    oc = []
    for p in range(C_W // LANES):
        cols = slice(p * LANES, (p + 1) * LANES)
        v_ones = jnp.concatenate([mv_ref[0, :, cols], ones], axis=1)
        acc, l, _ = _pair_attention(qc[:, cols], mk_ref[0, :, cols], v_ones, None)
        oc.append(acc * (1.0 / l))
    yc_ref[0] = (jnp.concatenate(oc, axis=1) * _silu(gc)).astype(bf16)

    qb, kb = proj(_DOT_GROUPS[2])
    _store_dilated(rope(qb) * Q_SCALE, qb_refs, stage_ref, stage4_ref)
    _store_dilated(rope(kb), kb_refs, stage_ref, stage4_ref)
    vb, gb = proj(_DOT_GROUPS[3])
    _store_dilated(vb, vb_refs, stage_ref, stage4_ref)
    gb_ref[0] = _silu(gb).astype(bf16)
    qa, ka = proj(_DOT_GROUPS[0])
    qa_ref[0] = (rope(qa) * Q_SCALE).astype(bf16)
    ka_ref[0] = _expand_kv(rope(ka)).astype(bf16)
    va, ga = proj(_DOT_GROUPS[1])
    va_ref[0] = _expand_kv(va).astype(bf16)
    ga_ref[0] = _silu(ga).astype(bf16)


def _in_proj(x, g, w, cos_t, sin_t, mk, mv):
    b, s, d = x.shape
    tm = IN_ROWS
    n_mem = mk.shape[1]
    row = lambda width: pl.BlockSpec((1, tm, width), lambda i, bi: (bi, i, 0))
    out_widths = (A_W, A_W, A_W, A_W, B_W, C_W)
    out_specs = [row(wd) for wd in out_widths]
    out_shape = [jax.ShapeDtypeStruct((b, s, wd), jnp.bfloat16) for wd in out_widths]
    for _ in range(3):
        for _, dil in B_CONFIGS:
            out_specs.append(pl.BlockSpec((1, dil, tm // dil, B_W), lambda i, bi: (bi, 0, i, 0)))
            out_shape.append(jax.ShapeDtypeStruct((b, dil, s // dil, B_W), jnp.bfloat16))
    stage = pltpu.VMEM((B_W // LANES, tm, LANES), jnp.float32)
    return pl.pallas_call(
        _in_proj_kernel,
        grid=(s // tm, b),
        in_specs=[
            row(d),
            pl.BlockSpec((1, d), lambda i, bi: (0, 0)),
            pl.BlockSpec((d, D_IN), lambda i, bi: (0, 0)),
            pl.BlockSpec((tm, LANES), lambda i, bi: (i, 0)),
            pl.BlockSpec((tm, LANES), lambda i, bi: (i, 0)),
            pl.BlockSpec((1, n_mem, C_W), lambda i, bi: (bi, 0, 0)),
            pl.BlockSpec((1, n_mem, C_W), lambda i, bi: (bi, 0, 0)),
        ],
        out_specs=out_specs,
        out_shape=out_shape,
        scratch_shapes=[stage, stage],
        compiler_params=pltpu.CompilerParams(
            dimension_semantics=("parallel", "parallel"), vmem_limit_bytes=VMEM_LIMIT),
        name="in_proj",
    )(x, g, w, cos_t, sin_t, mk, mv)


def _band_biases(max_dist, first_step, sink_ref, groups):
    qi = lax.broadcasted_iota(jnp.int32, (BLOCK, 2 * BLOCK), 0)
    kc = lax.broadcasted_iota(jnp.int32, (BLOCK, 2 * BLOCK), 1)
    dist = qi + BLOCK - kc
    band = (dist >= 0) & (dist <= max_dist)
    first_key = jnp.where(first_step, BLOCK, 0)
    neg = jnp.float32(-jnp.inf)
    bias_any = jnp.where(band, 0.0, neg)
    bias_first = jnp.where(band & (kc >= first_key), 0.0, neg)

    def pair_bias(base, p):
        if sink_ref is None:
            return jnp.concatenate([base, base], axis=0)
        assert max_dist < BLOCK
        halves = [jnp.where(kc == 0, sink_ref[2 * p + h] * LOG2E, base) for h in range(2)]
        return jnp.concatenate(halves, axis=0)

    if sink_ref is None:
        first, rest = pair_bias(bias_first, 0), pair_bias(bias_any, 0)
        return [first] * groups, [rest] * groups
    return ([pair_bias(bias_first, p) for p in range(groups)],
            [pair_bias(bias_any, p) for p in range(groups)])


def _band_chains(q_ref, kp_ref, ko_ref, vp_ref, vo_ref, bias_0, bias_rest, zero_row0, emit,
                 blocks=None):
    rows = q_ref.shape[2]
    groups = q_ref.shape[3] // LANES
    ones = jnp.ones((2 * BLOCK, LANES), jnp.bfloat16)
    tile = 16
    row0 = lax.broadcasted_iota(jnp.int32, (tile, 1), 0) == 0
    zero_first = lambda t: jnp.concatenate(
        [jnp.where(row0, jnp.zeros_like(t[:tile]), t[:tile]), t[tile:]], axis=0)
    for c in range(q_ref.shape[1]):
        for i in (range(rows // BLOCK) if blocks is None else blocks):
            r = slice(i * BLOCK, (i + 1) * BLOCK)
            q = q_ref[0, c, r, :]
            if i == 0:
                k_prev, v_prev = kp_ref[0, c], vp_ref[0, c]
            else:
                before = slice((i - 1) * BLOCK, i * BLOCK)
                k_prev, v_prev = ko_ref[0, c, before, :], vo_ref[0, c, before, :]
            if zero_row0:
                k_prev, v_prev = zero_first(k_prev), zero_first(v_prev)
            k = jnp.concatenate([k_prev, ko_ref[0, c, r, :]], axis=0)
            v = jnp.concatenate([v_prev, vo_ref[0, c, r, :]], axis=0)
            for p in range(groups):
                cols = slice(p * LANES, (p + 1) * LANES)
                v_ones = jnp.concatenate([v[:, cols], ones], axis=1)
                emit(c, i, p, *_pair_attention(q[:, cols], k[:, cols], v_ones,
                                               bias_0[p] if i == 0 else bias_rest[p]))


def _qkv_specs(dil, rows, width):
    n = rows // dil
    own = pl.BlockSpec((1, dil, n, width), lambda bi, j: (bi, 0, j, 0))
    prev = pl.BlockSpec(
        (1, dil, BLOCK, width), lambda bi, j: (bi, 0, jnp.maximum(j * (n // BLOCK) - 1, 0), 0))
    return [own, prev, own, prev, own]


def _attn_a_out_kernel(sink_ref, q_ref, kp_ref, ko_ref, vp_ref, vo_ref, gate_ref,
                       yb_ref, yc_ref, x_ref, w_ref, g_ref, out_ref):
    groups = q_ref.shape[3] // LANES
    bias_0, bias_rest = _band_biases(A_WINDOW - 1, pl.program_id(1) == 0, sink_ref, groups)
    tiles = {}

    def emit(c, i, p, acc, l, m):
        r = slice(i * BLOCK, (i + 1) * BLOCK)
        cols = slice(p * LANES, (p + 1) * LANES)
        gate = gate_ref[0, r, cols].astype(jnp.float32)
        tiles[i, p] = (acc * (1.0 / l) * gate).astype(jnp.bfloat16)

    per_group = OUT_GROUP // BLOCK
    for grp in range(q_ref.shape[2] // OUT_GROUP):
        blocks = range(grp * per_group, (grp + 1) * per_group)
        _band_chains(q_ref, kp_ref, ko_ref, vp_ref, vo_ref, bias_0, bias_rest, True, emit,
                     blocks=blocks)
        r = slice(grp * OUT_GROUP, (grp + 1) * OUT_GROUP)
        ya = jnp.concatenate(
            [jnp.concatenate([tiles[i, p] for p in range(groups)], axis=1) for i in blocks], axis=0)
        y = jnp.concatenate([ya, yb_ref[0, r, :], yc_ref[0, r, :]], axis=1)
        z = jnp.dot(y, w_ref[...], preferred_element_type=jnp.float32)
        out_ref[0, r, :] = x_ref[0, r, :] + z * _rms_scale(z) * g_ref[...]


def _attn_a_out(q, k, v, gate, sink, yb, yc, x, w, g):
    b, _, s, wq = q.shape
    d = x.shape[2]
    rows = OUT_ROWS
    row = lambda width: pl.BlockSpec((1, rows, width), lambda bi, j: (bi, j, 0))
    return pl.pallas_call(
        _attn_a_out_kernel,
        grid=(b, s // rows),
        in_specs=[pl.BlockSpec(memory_space=pltpu.SMEM)] + _qkv_specs(1, rows, wq)
        + [row(wq), row(yb.shape[2]), row(yc.shape[2]), row(d),
           pl.BlockSpec((d, d), lambda bi, j: (0, 0)),
           pl.BlockSpec((1, d), lambda bi, j: (0, 0))],
        out_specs=row(d),
        out_shape=jax.ShapeDtypeStruct((b, s, d), x.dtype),
        compiler_params=pltpu.CompilerParams(
            dimension_semantics=("parallel", "parallel"), vmem_limit_bytes=VMEM_LIMIT),
        name="attn_a_out",
    )(sink, q, k, k, v, v, gate, yb, yc, x, w, g)


def _stage_pitch(dil):
    pitch = dil if dil % 8 else dil + dil // 2
    assert pitch == dil or pitch % 8 == 0
    return pitch


def _stage_arrays(dil):
    return 3 if _stage_pitch(dil) == dil else 2


def _attn_b_kernel(*refs):
    n_cfg = len(B_CONFIGS)
    gate_ref, y_ref = refs[5 * n_cfg:5 * n_cfg + 2]
    stage = list(refs[5 * n_cfg + 2:])
    span = y_ref.shape[1]
    first_step = pl.program_id(1) == 0
    groups = B_W // LANES
    order = sorted(range(n_cfg), key=lambda c: -B_CONFIGS[c][1])
    assert B_CONFIGS[order[-1]][1] == 1
    qkv, biases, cover = {}, {}, {}
    for c in order:
        win, dil = B_CONFIGS[c]
        qkv[dil] = refs[5 * c:5 * c + 5]
        biases[dil] = _band_biases(win // dil, first_step, None, groups)
        cover[dil] = dil * BLOCK
    dils = [B_CONFIGS[c][1] for c in order]
    staged = {dil: tuple(stage.pop(0) for _ in range(_stage_arrays(dil))) for dil in dils[:-1]}

    def run(level, lo):
        dil = dils[level]
        blk = lo // cover[dil]

        def emit_staged(cls, i, p, acc, l, m):
            rows = pl.ds(cls, BLOCK, stride=_stage_pitch(dil))
            vals = (acc, l, m)
            if _stage_arrays(dil) == 2:
                vals = (acc * (1.0 / l), m + jnp.log(l) * LOG2E)
            for ref, val in zip(staged[dil], vals):
                ref[p, rows, :] = val

        def staged_rows(ref, coarse, p):
            pitch = _stage_pitch(coarse)
            run0 = (lo % cover[coarse]) // coarse
            if pitch == coarse:
                return ref[p, run0 * pitch:run0 * pitch + BLOCK, :]
            return jnp.concatenate(
                [ref[p, (run0 + g) * pitch:(run0 + g) * pitch + coarse, :]
                 for g in range(BLOCK // coarse)], axis=0)

        def emit_mixed(cls, i, p, acc, l, m):
            cols = slice(p * LANES, (p + 1) * LANES)
            accs, ls, ms = [acc], [l], [m]
            for coarse in dils[:-1]:
                vals = [staged_rows(ref, coarse, p) for ref in staged[coarse]]
                accs.append(vals[0])
                ls.append(vals[1] if len(vals) == 3 else None)
                ms.append(vals[-1])
            mx = functools.reduce(jnp.maximum, ms)
            wts = [jnp.exp2(mi - mx) for mi in ms]
            num = sum(w * a for w, a in zip(wts[1:], accs[1:])) + wts[0] * accs[0]
            den = sum((w if li is None else w * li) for w, li in zip(wts[1:], ls[1:])) + wts[0] * ls[0]
            r = slice(lo, lo + BLOCK)
            gate = gate_ref[0, r, cols].astype(jnp.float32)
            y_ref[0, r, cols] = (num * (1.0 / den) * gate).astype(y_ref.dtype)

        last = level == len(dils) - 1
        _band_chains(*qkv[dil], *biases[dil], False, emit_mixed if last else emit_staged,
                     blocks=(blk,))
        if not last:
            for sub in range(lo, lo + cover[dil], cover[dils[level + 1]]):
                run(level + 1, sub)

    for lo in range(0, span, cover[dils[0]]):
        run(0, lo)


def _attn_b(qkv_b, gate):
    n_cfg = len(B_CONFIGS)
    b, _, s, wq = qkv_b[0].shape
    rows = ATTN_ROWS
    in_specs, args = [], []
    for c, (_, dil) in enumerate(B_CONFIGS):
        q, k, v = qkv_b[c], qkv_b[n_cfg + c], qkv_b[2 * n_cfg + c]
        in_specs += _qkv_specs(dil, rows, wq)
        args += [q, k, k, v, v]
    row = pl.BlockSpec((1, rows, wq), lambda bi, j: (bi, j, 0))
    stages = []
    for dil in sorted((d for _, d in B_CONFIGS if d > 1), reverse=True):
        assert rows % (dil * BLOCK) == 0
        stages += [pltpu.VMEM((wq // LANES, _stage_pitch(dil) * BLOCK, LANES),
                              jnp.float32)] * _stage_arrays(dil)
    return pl.pallas_call(
        _attn_b_kernel,
        grid=(b, s // rows),
        in_specs=in_specs + [row],
        out_specs=row,
        out_shape=jax.ShapeDtypeStruct((b, s, wq), jnp.bfloat16),
        scratch_shapes=stages,
        compiler_params=pltpu.CompilerParams(
            dimension_semantics=("parallel", "parallel"), vmem_limit_bytes=ATTN_B_VMEM_LIMIT),
        name="attn_b",
    )(*args, gate)


def _rope_tables(seq):
    inv_freq = ROPE_THETA ** (-np.arange(0, ROT_DIM, 2, dtype=np.float64) / ROT_DIM)
    ang = np.arange(seq, dtype=np.float64)[:, None] * inv_freq[None, :]
    cos, sin = np.cos(ang), np.sin(ang)
    pad = HEAD_DIM - ROT_DIM
    cos_h = np.concatenate([cos, cos, np.ones((seq, pad))], axis=1)
    sin_h = np.concatenate([-sin, sin, np.zeros((seq, pad))], axis=1)
    reps = LANES // HEAD_DIM
    return (jnp.asarray(np.tile(cos_h, (1, reps)), jnp.float32),
            jnp.asarray(np.tile(sin_h, (1, reps)), jnp.float32))


def kernel(x, mem, pre_norm, w_in, sink_a, mem_norm, w_mem_kv, w_out, post_norm):
    depth = pre_norm.shape[0]
    cos_t, sin_t = _rope_tables(x.shape[1])
    bf16 = jnp.bfloat16
    h = x
    for l in range(depth):
        mk, mv = _mem_kv(mem, mem_norm[l][None], w_mem_kv[l].astype(bf16))
        qa, ka, va, ga, gb, yc, *qkv_b = _in_proj(
            h, pre_norm[l][None], w_in[l].astype(bf16), cos_t, sin_t, mk, mv)
        yb = _attn_b(qkv_b, gb)
        h = _attn_a_out(qa[:, None], ka[:, None], va[:, None], ga, sink_a[l], yb, yc, h,
                        w_out[l].astype(bf16), post_norm[l][None])
    return h
```

```python
import functools
import math

import jax
import jax.numpy as jnp
import numpy as np
from jax import lax
from jax.experimental import pallas as pl
from jax.experimental.pallas import tpu as pltpu

HEAD_DIM = 64
ROT_DIM = HEAD_DIM // 4
ROPE_THETA = 500000.0
BLOCK = 128
C_HEADS = 4
A_Q_HEADS = 6
A_KV_HEADS = 2
A_GROUP = A_Q_HEADS // A_KV_HEADS
A_WINDOW = 128
B_HEADS = 6
B_CONFIGS = ((128, 1), (512, 4), (2048, 16))
RMS_EPS = 1e-6

LANES = 128
A_W = A_Q_HEADS * HEAD_DIM
A_KV_W = A_KV_HEADS * HEAD_DIM
B_W = B_HEADS * HEAD_DIM
C_W = C_HEADS * HEAD_DIM

IN_ROWS = 1024
ATTN_ROWS = 2048
OUT_ROWS = 1024
OUT_GROUP = 1024
VMEM_LIMIT = 48 * 1024 * 1024
ATTN_B_VMEM_LIMIT = 56 * 1024 * 1024
IN_VMEM_LIMIT = 60 * 1024 * 1024

LOG2E = math.log2(math.e)
Q_SCALE = HEAD_DIM ** -0.5 * LOG2E


def _rms_scale(xf):
    return lax.rsqrt(jnp.mean(xf * xf, axis=-1, keepdims=True) + RMS_EPS)


def _silu(g):
    return g * (1.0 / (1.0 + jnp.exp(-g)))


def _left_half():
    return lax.broadcasted_iota(jnp.int32, (1, LANES), 1) < HEAD_DIM


def _pair_attention(q2, kg, v_ones, bias):
    rows = q2.shape[0]
    left = _left_half()
    keep_l = jnp.where(left, 1.0, 0.0).astype(q2.dtype)
    qs = jnp.concatenate([q2 * keep_l, q2 * (1 - keep_l)], axis=0)
    s = lax.dot_general(qs, kg, (((1,), (1,)), ((), ())), preferred_element_type=jnp.float32)
    if bias is not None:
        s = s + bias
    m = jnp.max(s, axis=-1, keepdims=True)
    e = jnp.exp2((s - m).astype(jnp.bfloat16))
    pv = jnp.dot(e, v_ones, preferred_element_type=jnp.float32)
    acc = jnp.where(left, pv[:rows, :LANES], pv[rows:, :LANES])
    l = jnp.where(left, pv[:rows, LANES:], pv[rows:, LANES:])
    m_own = jnp.where(left, m[:rows], m[rows:])
    return acc, l, m_own


def _mem_kv_kernel(mem_ref, g_ref, w_ref, mk_ref, mv_ref):
    m = mem_ref[0]
    u = (m * _rms_scale(m) * g_ref[...]).astype(jnp.bfloat16)
    kv = jnp.dot(u, w_ref[...], preferred_element_type=jnp.float32)
    mk_ref[0] = kv[:, :C_W].astype(jnp.bfloat16)
    mv_ref[0] = kv[:, C_W:].astype(jnp.bfloat16)


def _mem_kv(mem, g, w):
    b, n_mem, d = mem.shape
    return pl.pallas_call(
        _mem_kv_kernel,
        grid=(b,),
        in_specs=[
            pl.BlockSpec((1, n_mem, d), lambda i: (i, 0, 0)),
            pl.BlockSpec((1, d), lambda i: (0, 0)),
            pl.BlockSpec((d, 2 * C_W), lambda i: (0, 0)),
        ],
        out_specs=[
            pl.BlockSpec((1, n_mem, C_W), lambda i: (i, 0, 0)),
            pl.BlockSpec((1, n_mem, C_W), lambda i: (i, 0, 0)),
        ],
        out_shape=[jax.ShapeDtypeStruct((b, n_mem, C_W), jnp.bfloat16)] * 2,
        name="mem_kv",
    )(mem, g, w)


_IN_WIDTHS = (("qa", A_W), ("ka", A_KV_W), ("va", A_KV_W), ("ga", A_W),
              ("qb", B_W), ("kb", B_W), ("vb", B_W), ("gb", B_W), ("qc", C_W), ("gc", C_W))
D_IN = sum(w for _, w in _IN_WIDTHS)
_DOT_GROUPS = (("qa", "ka"), ("va", "ga"), ("qb", "kb"), ("vb", "gb"), ("qc", "gc"))


def _rope(t, cos_t, sin_t, first8):
    half = ROT_DIM // 2
    out = []
    for j in range(t.shape[1] // LANES):
        tj = t[:, j * LANES:(j + 1) * LANES]
        partner = jnp.where(first8, pltpu.roll(tj, LANES - half, 1), pltpu.roll(tj, half, 1))
        out.append(tj * cos_t + partner * sin_t)
    return out[0] if len(out) == 1 else jnp.concatenate(out, axis=1)


def _expand_kv(t):
    left = _left_half()
    swapped = pltpu.roll(t, HEAD_DIM, 1)
    groups = []
    for p in range(A_Q_HEADS // 2):
        kv_l, kv_r = (2 * p) // A_GROUP, (2 * p + 1) // A_GROUP
        if (kv_l, kv_r) == (0, 1):
            groups.append(t)
        elif kv_l == kv_r == 0:
            groups.append(jnp.where(left, t, swapped))
        else:
            groups.append(jnp.where(left, swapped, t))
    return jnp.concatenate(groups, axis=1)


def _store_dilated(t, refs, stage_ref, stage4_ref):
    rows = t.shape[0]
    slabs = t.shape[1] // LANES
    refs[1][0, 0] = t.astype(refs[1].dtype)
    for sl in range(slabs):
        stage_ref[sl] = t[:, sl * LANES:(sl + 1) * LANES]
    n4, n16 = rows // 4, rows // 16
    for sl in range(slabs):
        cols = slice(sl * LANES, (sl + 1) * LANES)
        for r in range(4):
            c4 = stage_ref[sl, pl.ds(r, n4, stride=4), :]
            refs[4][0, r, :, cols] = c4.astype(refs[4].dtype)
            stage4_ref[sl, r * n4:(r + 1) * n4, :] = c4
        for r in range(4):
            for q in range(4):
                c16 = stage4_ref[sl, pl.ds(r * n4 + q, n16, stride=4), :]
                refs[16][0, r + 4 * q, :, cols] = c16.astype(refs[16].dtype)


def _in_proj_kernel(x_ref, g_ref, w_ref, cos_ref, sin_ref, mk_ref, mv_ref,
                    qa_ref, ka_ref, va_ref, ga_ref, gb_ref, yc_ref, *rest):
    dils = [d for _, d in B_CONFIGS]
    qb_refs = dict(zip(dils, rest[0:3]))
    kb_refs = dict(zip(dils, rest[3:6]))
    vb_refs = dict(zip(dils, rest[6:9]))
    stage_ref, stage4_ref, wb_ref = rest[9:12]
    bf16 = jnp.bfloat16

    @pl.when((pl.program_id(0) == 0) & (pl.program_id(1) == 0))
    def _():
        for c0 in range(0, D_IN, 4 * LANES):
            wb_ref[:, c0:c0 + 4 * LANES] = w_ref[:, c0:c0 + 4 * LANES].astype(bf16)

    xf = x_ref[0]
    u = (xf * _rms_scale(xf) * g_ref[...]).astype(bf16)
    cos_t = cos_ref[...]
    sin_t = sin_ref[...]
    lane = lax.broadcasted_iota(jnp.int32, (1, LANES), 1)
    first8 = (lane % HEAD_DIM) < (ROT_DIM // 2)
    rope = lambda t: _rope(t, cos_t, sin_t, first8)

    offsets = {}
    off = 0
    for name, wd in _IN_WIDTHS:
        offsets[name] = (off, wd)
        off += wd

    def proj(names):
        c0 = offsets[names[0]][0]
        c1 = offsets[names[-1]][0] + offsets[names[-1]][1]
        full = jnp.dot(u, wb_ref[:, c0:c1], preferred_element_type=jnp.float32)
        return [full[:, offsets[n][0] - c0:offsets[n][0] - c0 + offsets[n][1]] for n in names]

    qc, gc = proj(_DOT_GROUPS[4])
    qc = (qc * Q_SCALE).astype(bf16)
    ones = jnp.ones((mv_ref.shape[1], LANES), bf16)
    oc = []
    for p in range(C_W // LANES):
        cols = slice(p * LANES, (p + 1) * LANES)
        v_ones = jnp.concatenate([mv_ref[0, :, cols], ones], axis=1)
        acc, l, _ = _pair_attention(qc[:, cols], mk_ref[0, :, cols], v_ones, None)
        oc.append(acc * (1.0 / l))
    yc_ref[0] = (jnp.concatenate(oc, axis=1) * _silu(gc)).astype(bf16)

    qb, kb = proj(_DOT_GROUPS[2])
    _store_dilated(rope(qb) * Q_SCALE, qb_refs, stage_ref, stage4_ref)
    _store_dilated(rope(kb), kb_refs, stage_ref, stage4_ref)
    vb, gb = proj(_DOT_GROUPS[3])
    _store_dilated(vb, vb_refs, stage_ref, stage4_ref)
    gb_ref[0] = _silu(gb).astype(bf16)
    qa, ka = proj(_DOT_GROUPS[0])
    qa_ref[0] = (rope(qa) * Q_SCALE).astype(bf16)
    ka_ref[0] = _expand_kv(rope(ka)).astype(bf16)
    va, ga = proj(_DOT_GROUPS[1])
    va_ref[0] = _expand_kv(va).astype(bf16)
    ga_ref[0] = _silu(ga).astype(bf16)


def _in_proj(x, g, w, cos_t, sin_t, mk, mv):
    b, s, d = x.shape
    tm = IN_ROWS
    n_mem = mk.shape[1]
    row = lambda width: pl.BlockSpec((1, tm, width), lambda i, bi: (bi, i, 0))
    out_widths = (A_W, A_W, A_W, A_W, B_W, C_W)
    out_specs = [row(wd) for wd in out_widths]
    out_shape = [jax.ShapeDtypeStruct((b, s, wd), jnp.bfloat16) for wd in out_widths]
    for _ in range(3):
        for _, dil in B_CONFIGS:
            out_specs.append(pl.BlockSpec((1, dil, tm // dil, B_W), lambda i, bi: (bi, 0, i, 0)))
            out_shape.append(jax.ShapeDtypeStruct((b, dil, s // dil, B_W), jnp.bfloat16))
    stage = pltpu.VMEM((B_W // LANES, tm, LANES), jnp.float32)
    return pl.pallas_call(
        _in_proj_kernel,
        grid=(s // tm, b),
        in_specs=[
            row(d),
            pl.BlockSpec((1, d), lambda i, bi: (0, 0)),
            pl.BlockSpec((d, D_IN), lambda i, bi: (0, 0), pipeline_mode=pl.Buffered(1)),
            pl.BlockSpec((tm, LANES), lambda i, bi: (i, 0)),
            pl.BlockSpec((tm, LANES), lambda i, bi: (i, 0)),
            pl.BlockSpec((1, n_mem, C_W), lambda i, bi: (bi, 0, 0)),
            pl.BlockSpec((1, n_mem, C_W), lambda i, bi: (bi, 0, 0)),
        ],
        out_specs=out_specs,
        out_shape=out_shape,
        scratch_shapes=[stage, stage, pltpu.VMEM((d, D_IN), jnp.bfloat16)],
        compiler_params=pltpu.CompilerParams(
            dimension_semantics=("arbitrary", "arbitrary"), vmem_limit_bytes=IN_VMEM_LIMIT),
        name="in_proj",
    )(x, g, w, cos_t, sin_t, mk, mv)


def _band_biases(max_dist, first_step, sink_ref, groups):
    qi = lax.broadcasted_iota(jnp.int32, (BLOCK, 2 * BLOCK), 0)
    kc = lax.broadcasted_iota(jnp.int32, (BLOCK, 2 * BLOCK), 1)
    dist = qi + BLOCK - kc
    band = (dist >= 0) & (dist <= max_dist)
    first_key = jnp.where(first_step, BLOCK, 0)
    neg = jnp.float32(-jnp.inf)
    bias_any = jnp.where(band, 0.0, neg)
    bias_first = jnp.where(band & (kc >= first_key), 0.0, neg)

    def pair_bias(base, p):
        if sink_ref is None:
            return jnp.concatenate([base, base], axis=0)
        assert max_dist < BLOCK
        halves = [jnp.where(kc == 0, sink_ref[2 * p + h] * LOG2E, base) for h in range(2)]
        return jnp.concatenate(halves, axis=0)

    if sink_ref is None:
        first, rest = pair_bias(bias_first, 0), pair_bias(bias_any, 0)
        return [first] * groups, [rest] * groups
    return ([pair_bias(bias_first, p) for p in range(groups)],
            [pair_bias(bias_any, p) for p in range(groups)])


def _band_chains(q_ref, kp_ref, ko_ref, vp_ref, vo_ref, bias_0, bias_rest, zero_row0, emit,
                 blocks=None):
    rows = q_ref.shape[2]
    groups = q_ref.shape[3] // LANES
    ones = jnp.ones((2 * BLOCK, LANES), jnp.bfloat16)
    tile = 16
    row0 = lax.broadcasted_iota(jnp.int32, (tile, 1), 0) == 0
    zero_first = lambda t: jnp.concatenate(
        [jnp.where(row0, jnp.zeros_like(t[:tile]), t[:tile]), t[tile:]], axis=0)
    for c in range(q_ref.shape[1]):
        for i in (range(rows // BLOCK) if blocks is None else blocks):
            r = slice(i * BLOCK, (i + 1) * BLOCK)
            q = q_ref[0, c, r, :]
            if i == 0:
                k_prev, v_prev = kp_ref[0, c], vp_ref[0, c]
            else:
                before = slice((i - 1) * BLOCK, i * BLOCK)
                k_prev, v_prev = ko_ref[0, c, before, :], vo_ref[0, c, before, :]
            if zero_row0:
                k_prev, v_prev = zero_first(k_prev), zero_first(v_prev)
            k = jnp.concatenate([k_prev, ko_ref[0, c, r, :]], axis=0)
            v = jnp.concatenate([v_prev, vo_ref[0, c, r, :]], axis=0)
            for p in range(groups):
                cols = slice(p * LANES, (p + 1) * LANES)
                v_ones = jnp.concatenate([v[:, cols], ones], axis=1)
                emit(c, i, p, *_pair_attention(q[:, cols], k[:, cols], v_ones,
                                               bias_0[p] if i == 0 else bias_rest[p]))


def _qkv_specs(dil, rows, width):
    n = rows // dil
    own = pl.BlockSpec((1, dil, n, width), lambda bi, j: (bi, 0, j, 0))
    prev = pl.BlockSpec(
        (1, dil, BLOCK, width), lambda bi, j: (bi, 0, jnp.maximum(j * (n // BLOCK) - 1, 0), 0))
    return [own, prev, own, prev, own]


def _attn_a_out_kernel(sink_ref, q_ref, kp_ref, ko_ref, vp_ref, vo_ref, gate_ref,
                       yb_ref, yc_ref, x_ref, w_ref, g_ref, out_ref, wb_ref):
    @pl.when((pl.program_id(0) == 0) & (pl.program_id(1) == 0))
    def _():
        wb_ref[...] = w_ref[...].astype(jnp.bfloat16)

    groups = q_ref.shape[3] // LANES
    bias_0, bias_rest = _band_biases(A_WINDOW - 1, pl.program_id(1) == 0, sink_ref, groups)
    tiles = {}

    def emit(c, i, p, acc, l, m):
        r = slice(i * BLOCK, (i + 1) * BLOCK)
        cols = slice(p * LANES, (p + 1) * LANES)
        gate = gate_ref[0, r, cols].astype(jnp.float32)
        tiles[i, p] = (acc * (1.0 / l) * gate).astype(jnp.bfloat16)

    per_group = OUT_GROUP // BLOCK
    for grp in range(q_ref.shape[2] // OUT_GROUP):
        blocks = range(grp * per_group, (grp + 1) * per_group)
        _band_chains(q_ref, kp_ref, ko_ref, vp_ref, vo_ref, bias_0, bias_rest, True, emit,
                     blocks=blocks)
        r = slice(grp * OUT_GROUP, (grp + 1) * OUT_GROUP)
        ya = jnp.concatenate(
            [jnp.concatenate([tiles[i, p] for p in range(groups)], axis=1) for i in blocks], axis=0)
        y = jnp.concatenate([ya, yb_ref[0, r, :], yc_ref[0, r, :]], axis=1)
        z = jnp.dot(y, wb_ref[...], preferred_element_type=jnp.float32)
        out_ref[0, r, :] = x_ref[0, r, :] + z * _rms_scale(z) * g_ref[...]


def _attn_a_out(q, k, v, gate, sink, yb, yc, x, w, g):
    b, _, s, wq = q.shape
    d = x.shape[2]
    rows = OUT_ROWS
    row = lambda width: pl.BlockSpec((1, rows, width), lambda bi, j: (bi, j, 0))
    return pl.pallas_call(
        _attn_a_out_kernel,
        grid=(b, s // rows),
        in_specs=[pl.BlockSpec(memory_space=pltpu.SMEM)] + _qkv_specs(1, rows, wq)
        + [row(wq), row(yb.shape[2]), row(yc.shape[2]), row(d),
           pl.BlockSpec((d, d), lambda bi, j: (0, 0), pipeline_mode=pl.Buffered(1)),
           pl.BlockSpec((1, d), lambda bi, j: (0, 0))],
        out_specs=row(d),
        out_shape=jax.ShapeDtypeStruct((b, s, d), x.dtype),
        scratch_shapes=[pltpu.VMEM((d, d), jnp.bfloat16)],
        compiler_params=pltpu.CompilerParams(
            dimension_semantics=("arbitrary", "arbitrary"), vmem_limit_bytes=VMEM_LIMIT),
        name="attn_a_out",
    )(sink, q, k, k, v, v, gate, yb, yc, x, w, g)


def _stage_pitch(dil):
    pitch = dil if dil % 8 else dil + dil // 2
    assert pitch == dil or pitch % 8 == 0
    return pitch


def _stage_arrays(dil):
    return 3 if _stage_pitch(dil) == dil else 2


def _attn_b_kernel(*refs):
    n_cfg = len(B_CONFIGS)
    gate_ref, y_ref = refs[5 * n_cfg:5 * n_cfg + 2]
    stage = list(refs[5 * n_cfg + 2:])
    span = y_ref.shape[1]
    first_step = pl.program_id(1) == 0
    groups = B_W // LANES
    order = sorted(range(n_cfg), key=lambda c: -B_CONFIGS[c][1])
    assert B_CONFIGS[order[-1]][1] == 1
    qkv, biases, cover = {}, {}, {}
    for c in order:
        win, dil = B_CONFIGS[c]
        qkv[dil] = refs[5 * c:5 * c + 5]
        biases[dil] = _band_biases(win // dil, first_step, None, groups)
        cover[dil] = dil * BLOCK
    dils = [B_CONFIGS[c][1] for c in order]
    staged = {dil: tuple(stage.pop(0) for _ in range(_stage_arrays(dil))) for dil in dils[:-1]}

    def run(level, lo):
        dil = dils[level]
        blk = lo // cover[dil]

        def emit_staged(cls, i, p, acc, l, m):
            rows = pl.ds(cls, BLOCK, stride=_stage_pitch(dil))
            vals = (acc, l, m)
            if _stage_arrays(dil) == 2:
                vals = (acc * (1.0 / l), m + jnp.log(l) * LOG2E)
            for ref, val in zip(staged[dil], vals):
                ref[p, rows, :] = val

        def staged_rows(ref, coarse, p):
            pitch = _stage_pitch(coarse)
            run0 = (lo % cover[coarse]) // coarse
            if pitch == coarse:
                return ref[p, run0 * pitch:run0 * pitch + BLOCK, :]
            return jnp.concatenate(
                [ref[p, (run0 + g) * pitch:(run0 + g) * pitch + coarse, :]
                 for g in range(BLOCK // coarse)], axis=0)

        def emit_mixed(cls, i, p, acc, l, m):
            cols = slice(p * LANES, (p + 1) * LANES)
            accs, ls, ms = [acc], [l], [m]
            for coarse in dils[:-1]:
                vals = [staged_rows(ref, coarse, p) for ref in staged[coarse]]
                accs.append(vals[0])
                ls.append(vals[1] if len(vals) == 3 else None)
                ms.append(vals[-1])
            mx = functools.reduce(jnp.maximum, ms)
            wts = [jnp.exp2(mi - mx) for mi in ms]
            num = sum(w * a for w, a in zip(wts[1:], accs[1:])) + wts[0] * accs[0]
            den = sum((w if li is None else w * li) for w, li in zip(wts[1:], ls[1:])) + wts[0] * ls[0]
            r = slice(lo, lo + BLOCK)
            gate = gate_ref[0, r, cols].astype(jnp.float32)
            y_ref[0, r, cols] = (num * (1.0 / den) * gate).astype(y_ref.dtype)

        last = level == len(dils) - 1
        _band_chains(*qkv[dil], *biases[dil], False, emit_mixed if last else emit_staged,
                     blocks=(blk,))
        if not last:
            for sub in range(lo, lo + cover[dil], cover[dils[level + 1]]):
                run(level + 1, sub)

    for lo in range(0, span, cover[dils[0]]):
        run(0, lo)


def _attn_b(qkv_b, gate):
    n_cfg = len(B_CONFIGS)
    b, _, s, wq = qkv_b[0].shape
    rows = ATTN_ROWS
    in_specs, args = [], []
    for c, (_, dil) in enumerate(B_CONFIGS):
        q, k, v = qkv_b[c], qkv_b[n_cfg + c], qkv_b[2 * n_cfg + c]
        in_specs += _qkv_specs(dil, rows, wq)
        args += [q, k, k, v, v]
    row = pl.BlockSpec((1, rows, wq), lambda bi, j: (bi, j, 0))
    stages = []
    for dil in sorted((d for _, d in B_CONFIGS if d > 1), reverse=True):
        assert rows % (dil * BLOCK) == 0
        stages += [pltpu.VMEM((wq // LANES, _stage_pitch(dil) * BLOCK, LANES),
                              jnp.float32)] * _stage_arrays(dil)
    return pl.pallas_call(
        _attn_b_kernel,
        grid=(b, s // rows),
        in_specs=in_specs + [row],
        out_specs=row,
        out_shape=jax.ShapeDtypeStruct((b, s, wq), jnp.bfloat16),
        scratch_shapes=stages,
        compiler_params=pltpu.CompilerParams(
            dimension_semantics=("parallel", "parallel"), vmem_limit_bytes=ATTN_B_VMEM_LIMIT),
        name="attn_b",
    )(*args, gate)


def _rope_tables(seq):
    inv_freq = ROPE_THETA ** (-np.arange(0, ROT_DIM, 2, dtype=np.float64) / ROT_DIM)
    ang = np.arange(seq, dtype=np.float64)[:, None] * inv_freq[None, :]
    cos, sin = np.cos(ang), np.sin(ang)
    pad = HEAD_DIM - ROT_DIM
    cos_h = np.concatenate([cos, cos, np.ones((seq, pad))], axis=1)
    sin_h = np.concatenate([-sin, sin, np.zeros((seq, pad))], axis=1)
    reps = LANES // HEAD_DIM
    return (jnp.asarray(np.tile(cos_h, (1, reps)), jnp.float32),
            jnp.asarray(np.tile(sin_h, (1, reps)), jnp.float32))


def kernel(x, mem, pre_norm, w_in, sink_a, mem_norm, w_mem_kv, w_out, post_norm):
    depth = pre_norm.shape[0]
    cos_t, sin_t = _rope_tables(x.shape[1])
    bf16 = jnp.bfloat16
    h = x
    for l in range(depth):
        mk, mv = _mem_kv(mem, mem_norm[l][None], w_mem_kv[l].astype(bf16))
        qa, ka, va, ga, gb, yc, *qkv_b = _in_proj(
            h, pre_norm[l][None], w_in[l], cos_t, sin_t, mk, mv)
        yb = _attn_b(qkv_b, gb)
        h = _attn_a_out(qa[:, None], ka[:, None], va[:, None], ga, sink_a[l], yb, yc, h,
                        w_out[l], post_norm[l][None])
    return h
```

```python
import functools
import math

import jax
import jax.numpy as jnp
import numpy as np
from jax import lax
from jax.experimental import pallas as pl
from jax.experimental.pallas import tpu as pltpu

HEAD_DIM = 64
ROT_DIM = HEAD_DIM // 4
ROPE_THETA = 500000.0
BLOCK = 128
C_HEADS = 4
A_Q_HEADS = 6
A_KV_HEADS = 2
A_GROUP = A_Q_HEADS // A_KV_HEADS
A_WINDOW = 128
B_HEADS = 6
B_CONFIGS = ((128, 1), (512, 4), (2048, 16))
RMS_EPS = 1e-6

LANES = 128
A_W = A_Q_HEADS * HEAD_DIM
A_KV_W = A_KV_HEADS * HEAD_DIM
B_W = B_HEADS * HEAD_DIM
C_W = C_HEADS * HEAD_DIM

IN_ROWS = 1024
IN_PARTS = 1
ATTN_ROWS = 2048
OUT_ROWS = 1024
OUT_GROUP = 1024
VMEM_LIMIT = 48 * 1024 * 1024
ATTN_B_VMEM_LIMIT = 56 * 1024 * 1024
IN_VMEM_LIMIT = 60 * 1024 * 1024

LOG2E = math.log2(math.e)
Q_SCALE = HEAD_DIM ** -0.5 * LOG2E


def _rms_scale(xf):
    return lax.rsqrt(jnp.mean(xf * xf, axis=-1, keepdims=True) + RMS_EPS)


def _silu(g):
    return g * (1.0 / (1.0 + jnp.exp(-g)))


def _left_half():
    return lax.broadcasted_iota(jnp.int32, (1, LANES), 1) < HEAD_DIM


def _pair_attention(q2, kg, v_ones, bias):
    rows = q2.shape[0]
    left = _left_half()
    keep_l = jnp.where(left, 1.0, 0.0).astype(q2.dtype)
    qs = jnp.concatenate([q2 * keep_l, q2 * (1 - keep_l)], axis=0)
    s = lax.dot_general(qs, kg, (((1,), (1,)), ((), ())), preferred_element_type=jnp.float32)
    if bias is not None:
        s = s + bias
    m = jnp.max(s, axis=-1, keepdims=True)
    e = jnp.exp2((s - m).astype(jnp.bfloat16))
    pv = jnp.dot(e, v_ones, preferred_element_type=jnp.float32)
    acc = jnp.where(left, pv[:rows, :LANES], pv[rows:, :LANES])
    l = jnp.where(left, pv[:rows, LANES:], pv[rows:, LANES:])
    m_own = jnp.where(left, m[:rows], m[rows:])
    return acc, l, m_own


def _mem_kv_kernel(mem_ref, g_ref, w_ref, mk_ref, mv_ref):
    m = mem_ref[0]
    u = (m * _rms_scale(m) * g_ref[...]).astype(jnp.bfloat16)
    kv = jnp.dot(u, w_ref[...].astype(jnp.bfloat16), preferred_element_type=jnp.float32)
    mk_ref[0] = kv[:, :C_W].astype(jnp.bfloat16)
    mv_ref[0] = kv[:, C_W:].astype(jnp.bfloat16)


def _mem_kv(mem, g, w):
    b, n_mem, d = mem.shape
    return pl.pallas_call(
        _mem_kv_kernel,
        grid=(b,),
        in_specs=[
            pl.BlockSpec((1, n_mem, d), lambda i: (i, 0, 0)),
            pl.BlockSpec((1, d), lambda i: (0, 0)),
            pl.BlockSpec((d, 2 * C_W), lambda i: (0, 0)),
        ],
        out_specs=[
            pl.BlockSpec((1, n_mem, C_W), lambda i: (i, 0, 0)),
            pl.BlockSpec((1, n_mem, C_W), lambda i: (i, 0, 0)),
        ],
        out_shape=[jax.ShapeDtypeStruct((b, n_mem, C_W), jnp.bfloat16)] * 2,
        name="mem_kv",
    )(mem, g, w)


_IN_WIDTHS = (("qa", A_W), ("ka", A_KV_W), ("va", A_KV_W), ("ga", A_W),
              ("qb", B_W), ("kb", B_W), ("vb", B_W), ("gb", B_W), ("qc", C_W), ("gc", C_W))
D_IN = sum(w for _, w in _IN_WIDTHS)
_DOT_GROUPS = (("qa", "ka"), ("va", "ga"), ("qb", "kb"), ("vb", "gb"), ("qc", "gc"))


def _rope(t, cos_t, sin_t, first8):
    half = ROT_DIM // 2
    out = []
    for j in range(t.shape[1] // LANES):
        tj = t[:, j * LANES:(j + 1) * LANES]
        partner = jnp.where(first8, pltpu.roll(tj, LANES - half, 1), pltpu.roll(tj, half, 1))
        out.append(tj * cos_t + partner * sin_t)
    return out[0] if len(out) == 1 else jnp.concatenate(out, axis=1)


def _expand_kv(t):
    left = _left_half()
    swapped = pltpu.roll(t, HEAD_DIM, 1)
    groups = []
    for p in range(A_Q_HEADS // 2):
        kv_l, kv_r = (2 * p) // A_GROUP, (2 * p + 1) // A_GROUP
        if (kv_l, kv_r) == (0, 1):
            groups.append(t)
        elif kv_l == kv_r == 0:
            groups.append(jnp.where(left, t, swapped))
        else:
            groups.append(jnp.where(left, swapped, t))
    return jnp.concatenate(groups, axis=1)


def _store_dilated(t, refs, part, stage_ref, stage4_ref):
    rows = t.shape[0]
    slabs = t.shape[1] // LANES
    refs[1][0, 0, part * rows:(part + 1) * rows, :] = t.astype(refs[1].dtype)
    for sl in range(slabs):
        stage_ref[sl] = t[:, sl * LANES:(sl + 1) * LANES]
    n4, n16 = rows // 4, rows // 16
    for sl in range(slabs):
        cols = slice(sl * LANES, (sl + 1) * LANES)
        for r in range(4):
            c4 = stage_ref[sl, pl.ds(r, n4, stride=4), :]
            refs[4][0, r, part * n4:(part + 1) * n4, cols] = c4.astype(refs[4].dtype)
            stage4_ref[sl, r * n4:(r + 1) * n4, :] = c4
        for r in range(4):
            for q in range(4):
                c16 = stage4_ref[sl, pl.ds(r * n4 + q, n16, stride=4), :]
                refs[16][0, r + 4 * q, part * n16:(part + 1) * n16, cols] = c16.astype(
                    refs[16].dtype)


def _in_proj_kernel(x_ref, g_ref, w_ref, cos_ref, sin_ref, mk_ref, mv_ref,
                    qa_ref, ka_ref, va_ref, ga_ref, gb_ref, yc_ref, *rest):
    dils = [d for _, d in B_CONFIGS]
    qb_refs = dict(zip(dils, rest[0:3]))
    kb_refs = dict(zip(dils, rest[3:6]))
    vb_refs = dict(zip(dils, rest[6:9]))
    wb_ref = rest[9]
    stages = rest[10:]
    bf16 = jnp.bfloat16

    @pl.when((pl.program_id(0) == 0) & (pl.program_id(1) == 0))
    def _():
        for c0 in range(0, D_IN, 4 * LANES):
            wb_ref[:, c0:c0 + 4 * LANES] = w_ref[:, c0:c0 + 4 * LANES].astype(bf16)

    lane = lax.broadcasted_iota(jnp.int32, (1, LANES), 1)
    first8 = (lane % HEAD_DIM) < (ROT_DIM // 2)
    offsets = {}
    off = 0
    for name, wd in _IN_WIDTHS:
        offsets[name] = (off, wd)
        off += wd
    ones = jnp.ones((mv_ref.shape[1], LANES), bf16)

    part_rows = x_ref.shape[1] // IN_PARTS
    for part in range(IN_PARTS):
        r = slice(part * part_rows, (part + 1) * part_rows)
        stage_ref, stage4_ref = stages[2 * part:2 * part + 2]
        xf = x_ref[0, r, :]
        u = (xf * _rms_scale(xf) * g_ref[...]).astype(bf16)
        cos_t = cos_ref[r, :]
        sin_t = sin_ref[r, :]
        rope = lambda t, cos_t=cos_t, sin_t=sin_t: _rope(t, cos_t, sin_t, first8)

        def proj(names, u=u):
            c0 = offsets[names[0]][0]
            c1 = offsets[names[-1]][0] + offsets[names[-1]][1]
            full = jnp.dot(u, wb_ref[:, c0:c1], preferred_element_type=jnp.float32)
            return [full[:, offsets[n][0] - c0:offsets[n][0] - c0 + offsets[n][1]] for n in names]

        qc, gc = proj(_DOT_GROUPS[4])
        qc = (qc * Q_SCALE).astype(bf16)
        oc = []
        for p in range(C_W // LANES):
            cols = slice(p * LANES, (p + 1) * LANES)
            v_ones = jnp.concatenate([mv_ref[0, :, cols], ones], axis=1)
            acc, l, _ = _pair_attention(qc[:, cols], mk_ref[0, :, cols], v_ones, None)
            oc.append(acc * (1.0 / l))
        yc_ref[0, r, :] = (jnp.concatenate(oc, axis=1) * _silu(gc)).astype(bf16)

        qb, kb = proj(_DOT_GROUPS[2])
        _store_dilated(rope(qb) * Q_SCALE, qb_refs, part, stage_ref, stage4_ref)
        _store_dilated(rope(kb), kb_refs, part, stage_ref, stage4_ref)
        vb, gb = proj(_DOT_GROUPS[3])
        _store_dilated(vb, vb_refs, part, stage_ref, stage4_ref)
        gb_ref[0, r, :] = _silu(gb).astype(bf16)
        qa, ka = proj(_DOT_GROUPS[0])
        qa_ref[0, r, :] = (rope(qa) * Q_SCALE).astype(bf16)
        ka_ref[0, r, :] = _expand_kv(rope(ka)).astype(bf16)
        va, ga = proj(_DOT_GROUPS[1])
        va_ref[0, r, :] = _expand_kv(va).astype(bf16)
        ga_ref[0, r, :] = _silu(ga).astype(bf16)


def _in_proj(x, g, w, cos_t, sin_t, mk, mv):
    b, s, d = x.shape
    tm = IN_ROWS
    n_mem = mk.shape[1]
    row = lambda width: pl.BlockSpec((1, tm, width), lambda i, bi: (bi, i, 0))
    out_widths = (A_W, A_W, A_W, A_W, B_W, C_W)
    out_specs = [row(wd) for wd in out_widths]
    out_shape = [jax.ShapeDtypeStruct((b, s, wd), jnp.bfloat16) for wd in out_widths]
    for _ in range(3):
        for _, dil in B_CONFIGS:
            out_specs.append(pl.BlockSpec((1, dil, tm // dil, B_W), lambda i, bi: (bi, 0, i, 0)))
            out_shape.append(jax.ShapeDtypeStruct((b, dil, s // dil, B_W), jnp.bfloat16))
    stage = pltpu.VMEM((B_W // LANES, tm // IN_PARTS, LANES), jnp.float32)
    return pl.pallas_call(
        _in_proj_kernel,
        grid=(s // tm, b),
        in_specs=[
            row(d),
            pl.BlockSpec((1, d), lambda i, bi: (0, 0)),
            pl.BlockSpec((d, D_IN), lambda i, bi: (0, 0), pipeline_mode=pl.Buffered(1)),
            pl.BlockSpec((tm, LANES), lambda i, bi: (i, 0)),
            pl.BlockSpec((tm, LANES), lambda i, bi: (i, 0)),
            pl.BlockSpec((1, n_mem, C_W), lambda i, bi: (bi, 0, 0)),
            pl.BlockSpec((1, n_mem, C_W), lambda i, bi: (bi, 0, 0)),
        ],
        out_specs=out_specs,
        out_shape=out_shape,
        scratch_shapes=[pltpu.VMEM((d, D_IN), jnp.bfloat16)] + [stage] * (2 * IN_PARTS),
        compiler_params=pltpu.CompilerParams(
            dimension_semantics=("arbitrary", "arbitrary"), vmem_limit_bytes=IN_VMEM_LIMIT),
        name="in_proj",
    )(x, g, w, cos_t, sin_t, mk, mv)


def _band_biases(max_dist, first_step, sink_ref, groups):
    qi = lax.broadcasted_iota(jnp.int32, (BLOCK, 2 * BLOCK), 0)
    kc = lax.broadcasted_iota(jnp.int32, (BLOCK, 2 * BLOCK), 1)
    dist = qi + BLOCK - kc
    band = (dist >= 0) & (dist <= max_dist)
    first_key = jnp.where(first_step, BLOCK, 0)
    neg = jnp.float32(-jnp.inf)
    bias_any = jnp.where(band, 0.0, neg)
    bias_first = jnp.where(band & (kc >= first_key), 0.0, neg)

    def pair_bias(base, p):
        if sink_ref is None:
            return jnp.concatenate([base, base], axis=0)
        assert max_dist < BLOCK
        halves = [jnp.where(kc == 0, sink_ref[2 * p + h] * LOG2E, base) for h in range(2)]
        return jnp.concatenate(halves, axis=0)

    if sink_ref is None:
        first, rest = pair_bias(bias_first, 0), pair_bias(bias_any, 0)
        return [first] * groups, [rest] * groups
    return ([pair_bias(bias_first, p) for p in range(groups)],
            [pair_bias(bias_any, p) for p in range(groups)])


def _band_chains(q_ref, kp_ref, ko_ref, vp_ref, vo_ref, bias_0, bias_rest, zero_row0, emit,
                 blocks=None):
    rows = q_ref.shape[2]
    groups = q_ref.shape[3] // LANES
    ones = jnp.ones((2 * BLOCK, LANES), jnp.bfloat16)
    tile = 16
    row0 = lax.broadcasted_iota(jnp.int32, (tile, 1), 0) == 0
    zero_first = lambda t: jnp.concatenate(
        [jnp.where(row0, jnp.zeros_like(t[:tile]), t[:tile]), t[tile:]], axis=0)
    for c in range(q_ref.shape[1]):
        for i in (range(rows // BLOCK) if blocks is None else blocks):
            r = slice(i * BLOCK, (i + 1) * BLOCK)
            q = q_ref[0, c, r, :]
            if i == 0:
                k_prev, v_prev = kp_ref[0, c], vp_ref[0, c]
            else:
                before = slice((i - 1) * BLOCK, i * BLOCK)
                k_prev, v_prev = ko_ref[0, c, before, :], vo_ref[0, c, before, :]
            if zero_row0:
                k_prev, v_prev = zero_first(k_prev), zero_first(v_prev)
            k = jnp.concatenate([k_prev, ko_ref[0, c, r, :]], axis=0)
            v = jnp.concatenate([v_prev, vo_ref[0, c, r, :]], axis=0)
            for p in range(groups):
                cols = slice(p * LANES, (p + 1) * LANES)
                v_ones = jnp.concatenate([v[:, cols], ones], axis=1)
                emit(c, i, p, *_pair_attention(q[:, cols], k[:, cols], v_ones,
                                               bias_0[p] if i == 0 else bias_rest[p]))


def _qkv_specs(dil, rows, width):
    n = rows // dil
    own = pl.BlockSpec((1, dil, n, width), lambda bi, j: (bi, 0, j, 0))
    prev = pl.BlockSpec(
        (1, dil, BLOCK, width), lambda bi, j: (bi, 0, jnp.maximum(j * (n // BLOCK) - 1, 0), 0))
    return [own, prev, own, prev, own]


def _attn_a_out_kernel(sink_ref, q_ref, kp_ref, ko_ref, vp_ref, vo_ref, gate_ref,
                       yb_ref, yc_ref, x_ref, w_ref, g_ref, out_ref, wb_ref):
    @pl.when((pl.program_id(0) == 0) & (pl.program_id(1) == 0))
    def _():
        wb_ref[...] = w_ref[...].astype(jnp.bfloat16)

    groups = q_ref.shape[3] // LANES
    bias_0, bias_rest = _band_biases(A_WINDOW - 1, pl.program_id(1) == 0, sink_ref, groups)
    tiles = {}

    def emit(c, i, p, acc, l, m):
        r = slice(i * BLOCK, (i + 1) * BLOCK)
        cols = slice(p * LANES, (p + 1) * LANES)
        gate = gate_ref[0, r, cols].astype(jnp.float32)
        tiles[i, p] = (acc * (1.0 / l) * gate).astype(jnp.bfloat16)

    per_group = OUT_GROUP // BLOCK
    for grp in range(q_ref.shape[2] // OUT_GROUP):
        blocks = range(grp * per_group, (grp + 1) * per_group)
        _band_chains(q_ref, kp_ref, ko_ref, vp_ref, vo_ref, bias_0, bias_rest, True, emit,
                     blocks=blocks)
        r = slice(grp * OUT_GROUP, (grp + 1) * OUT_GROUP)
        ya = jnp.concatenate(
            [jnp.concatenate([tiles[i, p] for p in range(groups)], axis=1) for i in blocks], axis=0)
        y = jnp.concatenate([ya, yb_ref[0, r, :], yc_ref[0, r, :]], axis=1)
        z = jnp.dot(y, wb_ref[...], preferred_element_type=jnp.float32)
        out_ref[0, r, :] = x_ref[0, r, :] + z * _rms_scale(z) * g_ref[...]


def _attn_a_out(q, k, v, gate, sink, yb, yc, x, w, g):
    b, _, s, wq = q.shape
    d = x.shape[2]
    rows = OUT_ROWS
    row = lambda width: pl.BlockSpec((1, rows, width), lambda bi, j: (bi, j, 0))
    return pl.pallas_call(
        _attn_a_out_kernel,
        grid=(b, s // rows),
        in_specs=[pl.BlockSpec(memory_space=pltpu.SMEM)] + _qkv_specs(1, rows, wq)
        + [row(wq), row(yb.shape[2]), row(yc.shape[2]), row(d),
           pl.BlockSpec((d, d), lambda bi, j: (0, 0), pipeline_mode=pl.Buffered(1)),
           pl.BlockSpec((1, d), lambda bi, j: (0, 0))],
        out_specs=row(d),
        out_shape=jax.ShapeDtypeStruct((b, s, d), x.dtype),
        scratch_shapes=[pltpu.VMEM((d, d), jnp.bfloat16)],
        compiler_params=pltpu.CompilerParams(
            dimension_semantics=("arbitrary", "arbitrary"), vmem_limit_bytes=VMEM_LIMIT),
        name="attn_a_out",
    )(sink, q, k, k, v, v, gate, yb, yc, x, w, g)


def _stage_pitch(dil):
    pitch = dil if dil % 8 else dil + dil // 2
    assert pitch == dil or pitch % 8 == 0
    return pitch


def _stage_arrays(dil):
    return 3 if _stage_pitch(dil) == dil else 2


def _attn_b_kernel(*refs):
    n_cfg = len(B_CONFIGS)
    gate_ref, y_ref = refs[5 * n_cfg:5 * n_cfg + 2]
    stage = list(refs[5 * n_cfg + 2:])
    span = y_ref.shape[1]
    first_step = pl.program_id(1) == 0
    groups = B_W // LANES
    order = sorted(range(n_cfg), key=lambda c: -B_CONFIGS[c][1])
    assert B_CONFIGS[order[-1]][1] == 1
    qkv, biases, cover = {}, {}, {}
    for c in order:
        win, dil = B_CONFIGS[c]
        qkv[dil] = refs[5 * c:5 * c + 5]
        biases[dil] = _band_biases(win // dil, first_step, None, groups)
        cover[dil] = dil * BLOCK
    dils = [B_CONFIGS[c][1] for c in order]
    staged = {dil: tuple(stage.pop(0) for _ in range(_stage_arrays(dil))) for dil in dils[:-1]}

    def run(level, lo):
        dil = dils[level]
        blk = lo // cover[dil]

        def emit_staged(cls, i, p, acc, l, m):
            rows = pl.ds(cls, BLOCK, stride=_stage_pitch(dil))
            vals = (acc, l, m)
            if _stage_arrays(dil) == 2:
                vals = (acc * (1.0 / l), m + jnp.log(l) * LOG2E)
            for ref, val in zip(staged[dil], vals):
                ref[p, rows, :] = val

        def staged_rows(ref, coarse, p):
            pitch = _stage_pitch(coarse)
            run0 = (lo % cover[coarse]) // coarse
            if pitch == coarse:
                return ref[p, run0 * pitch:run0 * pitch + BLOCK, :]
            return jnp.concatenate(
                [ref[p, (run0 + g) * pitch:(run0 + g) * pitch + coarse, :]
                 for g in range(BLOCK // coarse)], axis=0)

        def emit_mixed(cls, i, p, acc, l, m):
            cols = slice(p * LANES, (p + 1) * LANES)
            accs, ls, ms = [acc], [l], [m]
            for coarse in dils[:-1]:
                vals = [staged_rows(ref, coarse, p) for ref in staged[coarse]]
                accs.append(vals[0])
                ls.append(vals[1] if len(vals) == 3 else None)
                ms.append(vals[-1])
            mx = functools.reduce(jnp.maximum, ms)
            wts = [jnp.exp2(mi - mx) for mi in ms]
            num = sum(w * a for w, a in zip(wts[1:], accs[1:])) + wts[0] * accs[0]
            den = sum((w if li is None else w * li) for w, li in zip(wts[1:], ls[1:])) + wts[0] * ls[0]
            r = slice(lo, lo + BLOCK)
            gate = gate_ref[0, r, cols].astype(jnp.float32)
            y_ref[0, r, cols] = (num * (1.0 / den) * gate).astype(y_ref.dtype)

        last = level == len(dils) - 1
        _band_chains(*qkv[dil], *biases[dil], False, emit_mixed if last else emit_staged,
                     blocks=(blk,))
        if not last:
            for sub in range(lo, lo + cover[dil], cover[dils[level + 1]]):
                run(level + 1, sub)

    for lo in range(0, span, cover[dils[0]]):
        run(0, lo)


def _attn_b(qkv_b, gate):
    n_cfg = len(B_CONFIGS)
    b, _, s, wq = qkv_b[0].shape
    rows = ATTN_ROWS
    in_specs, args = [], []
    for c, (_, dil) in enumerate(B_CONFIGS):
        q, k, v = qkv_b[c], qkv_b[n_cfg + c], qkv_b[2 * n_cfg + c]
        in_specs += _qkv_specs(dil, rows, wq)
        args += [q, k, k, v, v]
    row = pl.BlockSpec((1, rows, wq), lambda bi, j: (bi, j, 0))
    stages = []
    for dil in sorted((d for _, d in B_CONFIGS if d > 1), reverse=True):
        assert rows % (dil * BLOCK) == 0
        stages += [pltpu.VMEM((wq // LANES, _stage_pitch(dil) * BLOCK, LANES),
                              jnp.float32)] * _stage_arrays(dil)
    return pl.pallas_call(
        _attn_b_kernel,
        grid=(b, s // rows),
        in_specs=in_specs + [row],
        out_specs=row,
        out_shape=jax.ShapeDtypeStruct((b, s, wq), jnp.bfloat16),
        scratch_shapes=stages,
        compiler_params=pltpu.CompilerParams(
            dimension_semantics=("parallel", "parallel"), vmem_limit_bytes=ATTN_B_VMEM_LIMIT),
        name="attn_b",
    )(*args, gate)


def _rope_tables(seq):
    inv_freq = ROPE_THETA ** (-np.arange(0, ROT_DIM, 2, dtype=np.float64) / ROT_DIM)
    ang = np.arange(seq, dtype=np.float64)[:, None] * inv_freq[None, :]
    cos, sin = np.cos(ang), np.sin(ang)
    pad = HEAD_DIM - ROT_DIM
    cos_h = np.concatenate([cos, cos, np.ones((seq, pad))], axis=1)
    sin_h = np.concatenate([-sin, sin, np.zeros((seq, pad))], axis=1)
    reps = LANES // HEAD_DIM
    return (jnp.asarray(np.tile(cos_h, (1, reps)), jnp.float32),
            jnp.asarray(np.tile(sin_h, (1, reps)), jnp.float32))


def kernel(x, mem, pre_norm, w_in, sink_a, mem_norm, w_mem_kv, w_out, post_norm):
    depth = pre_norm.shape[0]
    cos_t, sin_t = _rope_tables(x.shape[1])
    h = x
    for l in range(depth):
        mk, mv = _mem_kv(mem, mem_norm[l][None], w_mem_kv[l])
        qa, ka, va, ga, gb, yc, *qkv_b = _in_proj(
            h, pre_norm[l][None], w_in[l], cos_t, sin_t, mk, mv)
        yb = _attn_b(qkv_b, gb)
        h = _attn_a_out(qa[:, None], ka[:, None], va[:, None], ga, sink_a[l], yb, yc, h,
                        w_out[l], post_norm[l][None])
    return h
```

```python
import functools
import itertools
import math

import jax
import jax.numpy as jnp
import numpy as np
from jax import lax
from jax.experimental import pallas as pl
from jax.experimental.pallas import tpu as pltpu

HEAD_DIM = 64
ROT_DIM = HEAD_DIM // 4
ROPE_THETA = 500000.0
BLOCK = 128
C_HEADS = 4
A_Q_HEADS = 6
A_KV_HEADS = 2
A_GROUP = A_Q_HEADS // A_KV_HEADS
A_WINDOW = 128
B_HEADS = 6
B_CONFIGS = ((128, 1), (512, 4), (2048, 16))
RMS_EPS = 1e-6

LANES = 128
A_W = A_Q_HEADS * HEAD_DIM
A_KV_W = A_KV_HEADS * HEAD_DIM
B_W = B_HEADS * HEAD_DIM
C_W = C_HEADS * HEAD_DIM

IN_ROWS = 1024
IN_PARTS = 1
ATTN_ROWS = 2048
OUT_ROWS = 1024
OUT_GROUP = 1024
VMEM_LIMIT = 48 * 1024 * 1024
ATTN_B_VMEM_LIMIT = 56 * 1024 * 1024
IN_VMEM_LIMIT = 60 * 1024 * 1024

MASKED = -3e38
LOG2E = math.log2(math.e)
Q_SCALE = HEAD_DIM ** -0.5 * LOG2E


def _rms_scale(xf):
    return lax.rsqrt(jnp.mean(xf * xf, axis=-1, keepdims=True) + RMS_EPS)


def _silu(g):
    return g * (1.0 / (1.0 + jnp.exp(-g)))


def _left_half():
    return lax.broadcasted_iota(jnp.int32, (1, LANES), 1) < HEAD_DIM


def _pair_attention(q2, kg, v_ones, bias, bias_keys=None):
    rows = q2.shape[0]
    left = _left_half()
    keep_l = jnp.where(left, 1.0, 0.0).astype(q2.dtype)
    qs = jnp.concatenate([q2 * keep_l, q2 * (1 - keep_l)], axis=0)
    if bias_keys is not None:
        mask_t, onehot = bias_keys
        qs = jnp.concatenate([qs, onehot], axis=1)
        kg = jnp.concatenate([kg, mask_t], axis=1)
    s = lax.dot_general(qs, kg, (((1,), (1,)), ((), ())), preferred_element_type=jnp.float32)
    if bias is not None:
        s = s + bias
    m = jnp.max(s, axis=-1, keepdims=True)
    e = jnp.exp2((s - m).astype(jnp.bfloat16))
    pv = jnp.dot(e, v_ones, preferred_element_type=jnp.float32)
    acc = jnp.where(left, pv[:rows, :LANES], pv[rows:, :LANES])
    l = jnp.where(left, pv[:rows, LANES:], pv[rows:, LANES:])
    m_own = jnp.where(left, m[:rows], m[rows:])
    return acc, l, m_own


def _mem_kv_kernel(mem_ref, g_ref, w_ref, mk_ref, mv_ref):
    m = mem_ref[0]
    u = (m * _rms_scale(m) * g_ref[...]).astype(jnp.bfloat16)
    kv = jnp.dot(u, w_ref[...].astype(jnp.bfloat16), preferred_element_type=jnp.float32)
    mk_ref[0] = kv[:, :C_W].astype(jnp.bfloat16)
    mv_ref[0] = kv[:, C_W:].astype(jnp.bfloat16)


def _mem_kv(mem, g, w):
    b, n_mem, d = mem.shape
    return pl.pallas_call(
        _mem_kv_kernel,
        grid=(b,),
        in_specs=[
            pl.BlockSpec((1, n_mem, d), lambda i: (i, 0, 0)),
            pl.BlockSpec((1, d), lambda i: (0, 0)),
            pl.BlockSpec((d, 2 * C_W), lambda i: (0, 0)),
        ],
        out_specs=[
            pl.BlockSpec((1, n_mem, C_W), lambda i: (i, 0, 0)),
            pl.BlockSpec((1, n_mem, C_W), lambda i: (i, 0, 0)),
        ],
        out_shape=[jax.ShapeDtypeStruct((b, n_mem, C_W), jnp.bfloat16)] * 2,
        name="mem_kv",
    )(mem, g, w)


_IN_WIDTHS = (("qa", A_W), ("ka", A_KV_W), ("va", A_KV_W), ("ga", A_W),
              ("qb", B_W), ("kb", B_W), ("vb", B_W), ("gb", B_W), ("qc", C_W), ("gc", C_W))
D_IN = sum(w for _, w in _IN_WIDTHS)
_DOT_GROUPS = (("qa", "ka"), ("va", "ga"), ("qb", "kb"), ("vb", "gb"), ("qc", "gc"))


def _rope(t, cos_t, sin_t, first8):
    half = ROT_DIM // 2
    out = []
    for j in range(t.shape[1] // LANES):
        tj = t[:, j * LANES:(j + 1) * LANES]
        partner = jnp.where(first8, pltpu.roll(tj, LANES - half, 1), pltpu.roll(tj, half, 1))
        out.append(tj * cos_t + partner * sin_t)
    return out[0] if len(out) == 1 else jnp.concatenate(out, axis=1)


def _expand_kv(t):
    left = _left_half()
    swapped = pltpu.roll(t, HEAD_DIM, 1)
    groups = []
    for p in range(A_Q_HEADS // 2):
        kv_l, kv_r = (2 * p) // A_GROUP, (2 * p + 1) // A_GROUP
        if (kv_l, kv_r) == (0, 1):
            groups.append(t)
        elif kv_l == kv_r == 0:
            groups.append(jnp.where(left, t, swapped))
        else:
            groups.append(jnp.where(left, swapped, t))
    return jnp.concatenate(groups, axis=1)


def _store_dilated(t, refs, part, stage_ref, stage4_ref):
    rows = t.shape[0]
    slabs = t.shape[1] // LANES
    refs[1][0, 0, part * rows:(part + 1) * rows, :] = t.astype(refs[1].dtype)
    for sl in range(slabs):
        stage_ref[sl] = t[:, sl * LANES:(sl + 1) * LANES]
    n4, n16 = rows // 4, rows // 16
    for sl in range(slabs):
        cols = slice(sl * LANES, (sl + 1) * LANES)
        for r in range(4):
            c4 = stage_ref[sl, pl.ds(r, n4, stride=4), :]
            refs[4][0, r, part * n4:(part + 1) * n4, cols] = c4.astype(refs[4].dtype)
            stage4_ref[sl, r * n4:(r + 1) * n4, :] = c4
        for r in range(4):
            for q in range(4):
                c16 = stage4_ref[sl, pl.ds(r * n4 + q, n16, stride=4), :]
                refs[16][0, r + 4 * q, part * n16:(part + 1) * n16, cols] = c16.astype(
                    refs[16].dtype)


def _in_proj_kernel(x_ref, g_ref, w_ref, cos_ref, sin_ref, mk_ref, mv_ref,
                    qa_ref, ka_ref, va_ref, ga_ref, gb_ref, yc_ref, *rest):
    dils = [d for _, d in B_CONFIGS]
    qb_refs = dict(zip(dils, rest[0:3]))
    kb_refs = dict(zip(dils, rest[3:6]))
    vb_refs = dict(zip(dils, rest[6:9]))
    wb_ref = rest[9]
    stages = rest[10:]
    bf16 = jnp.bfloat16

    @pl.when((pl.program_id(0) == 0) & (pl.program_id(1) == 0))
    def _():
        for c0 in range(0, D_IN, 4 * LANES):
            wb_ref[:, c0:c0 + 4 * LANES] = w_ref[:, c0:c0 + 4 * LANES].astype(bf16)

    lane = lax.broadcasted_iota(jnp.int32, (1, LANES), 1)
    first8 = (lane % HEAD_DIM) < (ROT_DIM // 2)
    offsets = {}
    off = 0
    for name, wd in _IN_WIDTHS:
        offsets[name] = (off, wd)
        off += wd
    ones = jnp.ones((mv_ref.shape[1], LANES), bf16)

    part_rows = x_ref.shape[1] // IN_PARTS
    for part in range(IN_PARTS):
        r = slice(part * part_rows, (part + 1) * part_rows)
        stage_ref, stage4_ref = stages[2 * part:2 * part + 2]
        xf = x_ref[0, r, :]
        u = (xf * _rms_scale(xf) * g_ref[...]).astype(bf16)
        cos_t = cos_ref[r, :]
        sin_t = sin_ref[r, :]
        rope = lambda t, cos_t=cos_t, sin_t=sin_t: _rope(t, cos_t, sin_t, first8)

        def proj(names, u=u):
            c0 = offsets[names[0]][0]
            c1 = offsets[names[-1]][0] + offsets[names[-1]][1]
            full = jnp.dot(u, wb_ref[:, c0:c1], preferred_element_type=jnp.float32)
            return [full[:, offsets[n][0] - c0:offsets[n][0] - c0 + offsets[n][1]] for n in names]

        qc, gc = proj(_DOT_GROUPS[4])
        qc = (qc * Q_SCALE).astype(bf16)
        oc = []
        for p in range(C_W // LANES):
            cols = slice(p * LANES, (p + 1) * LANES)
            v_ones = jnp.concatenate([mv_ref[0, :, cols], ones], axis=1)
            acc, l, _ = _pair_attention(qc[:, cols], mk_ref[0, :, cols], v_ones, None)
            oc.append(acc * (1.0 / l))
        yc_ref[0, r, :] = (jnp.concatenate(oc, axis=1) * _silu(gc)).astype(bf16)

        qb, kb = proj(_DOT_GROUPS[2])
        _store_dilated(rope(qb) * Q_SCALE, qb_refs, part, stage_ref, stage4_ref)
        _store_dilated(rope(kb), kb_refs, part, stage_ref, stage4_ref)
        vb, gb = proj(_DOT_GROUPS[3])
        _store_dilated(vb, vb_refs, part, stage_ref, stage4_ref)
        gb_ref[0, r, :] = _silu(gb).astype(bf16)
        qa, ka = proj(_DOT_GROUPS[0])
        qa_ref[0, r, :] = (rope(qa) * Q_SCALE).astype(bf16)
        ka_ref[0, r, :] = _expand_kv(rope(ka)).astype(bf16)
        va, ga = proj(_DOT_GROUPS[1])
        va_ref[0, r, :] = _expand_kv(va).astype(bf16)
        ga_ref[0, r, :] = _silu(ga).astype(bf16)


def _in_proj(x, g, w, cos_t, sin_t, mk, mv):
    b, s, d = x.shape
    tm = IN_ROWS
    n_mem = mk.shape[1]
    row = lambda width: pl.BlockSpec((1, tm, width), lambda i, bi: (bi, i, 0))
    out_widths = (A_W, A_W, A_W, A_W, B_W, C_W)
    out_specs = [row(wd) for wd in out_widths]
    out_shape = [jax.ShapeDtypeStruct((b, s, wd), jnp.bfloat16) for wd in out_widths]
    for _ in range(3):
        for _, dil in B_CONFIGS:
            out_specs.append(pl.BlockSpec((1, dil, tm // dil, B_W), lambda i, bi: (bi, 0, i, 0)))
            out_shape.append(jax.ShapeDtypeStruct((b, dil, s // dil, B_W), jnp.bfloat16))
    stage = pltpu.VMEM((B_W // LANES, tm // IN_PARTS, LANES), jnp.float32)
    return pl.pallas_call(
        _in_proj_kernel,
        grid=(s // tm, b),
        in_specs=[
            row(d),
            pl.BlockSpec((1, d), lambda i, bi: (0, 0)),
            pl.BlockSpec((d, D_IN), lambda i, bi: (0, 0), pipeline_mode=pl.Buffered(1)),
            pl.BlockSpec((tm, LANES), lambda i, bi: (i, 0)),
            pl.BlockSpec((tm, LANES), lambda i, bi: (i, 0)),
            pl.BlockSpec((1, n_mem, C_W), lambda i, bi: (bi, 0, 0)),
            pl.BlockSpec((1, n_mem, C_W), lambda i, bi: (bi, 0, 0)),
        ],
        out_specs=out_specs,
        out_shape=out_shape,
        scratch_shapes=[pltpu.VMEM((d, D_IN), jnp.bfloat16)] + [stage] * (2 * IN_PARTS),
        compiler_params=pltpu.CompilerParams(
            dimension_semantics=("arbitrary", "arbitrary"), vmem_limit_bytes=IN_VMEM_LIMIT),
        name="in_proj",
    )(x, g, w, cos_t, sin_t, mk, mv)


def _band_biases(max_dist, first_step, sink_ref, groups):
    qi = lax.broadcasted_iota(jnp.int32, (BLOCK, 2 * BLOCK), 0)
    kc = lax.broadcasted_iota(jnp.int32, (BLOCK, 2 * BLOCK), 1)
    dist = qi + BLOCK - kc
    band = (dist >= 0) & (dist <= max_dist)
    first_key = jnp.where(first_step, BLOCK, 0)
    neg = jnp.float32(-jnp.inf)
    bias_any = jnp.where(band, 0.0, neg)
    bias_first = jnp.where(band & (kc >= first_key), 0.0, neg)

    def pair_bias(base, p):
        if sink_ref is None:
            return jnp.concatenate([base, base], axis=0)
        assert max_dist < BLOCK
        halves = [jnp.where(kc == 0, sink_ref[2 * p + h] * LOG2E, base) for h in range(2)]
        return jnp.concatenate(halves, axis=0)

    if sink_ref is None:
        first, rest = pair_bias(bias_first, 0), pair_bias(bias_any, 0)
        return [first] * groups, [rest] * groups
    return ([pair_bias(bias_first, p) for p in range(groups)],
            [pair_bias(bias_any, p) for p in range(groups)])


def _band_mask_keys(max_dist, first_step, groups):
    kc = lax.broadcasted_iota(jnp.int32, (2 * BLOCK, BLOCK), 0)
    qi = lax.broadcasted_iota(jnp.int32, (2 * BLOCK, BLOCK), 1)
    dist = qi + BLOCK - kc
    band = (dist >= 0) & (dist <= max_dist)
    first_key = jnp.where(first_step, BLOCK, 0)
    rest = jnp.where(band, 0.0, MASKED).astype(jnp.bfloat16)
    first = jnp.where(band & (kc >= first_key), 0.0, MASKED).astype(jnp.bfloat16)
    row = lax.broadcasted_iota(jnp.int32, (2 * BLOCK, BLOCK), 0)
    onehot = jnp.where(row % BLOCK == qi, 1.0, 0.0).astype(jnp.bfloat16)
    return [(first, onehot)] * groups, [(rest, onehot)] * groups


def _band_chains(*args, **kwargs):
    for _ in _band_chain_steps(*args, **kwargs):
        pass


def _interleave(*step_iters):
    iters = [iter(s) for s in step_iters]
    while iters:
        iters = [it for it in iters if next(it, StopIteration) is not StopIteration]


def _band_chain_steps(q_ref, kp_ref, ko_ref, vp_ref, vo_ref, bias_0, bias_rest, zero_row0, emit,
                      blocks=None):
    rows = q_ref.shape[2]
    groups = q_ref.shape[3] // LANES
    ones = jnp.ones((2 * BLOCK, LANES), jnp.bfloat16)
    tile = 16
    row0 = lax.broadcasted_iota(jnp.int32, (tile, 1), 0) == 0
    zero_first = lambda t: jnp.concatenate(
        [jnp.where(row0, jnp.zeros_like(t[:tile]), t[:tile]), t[tile:]], axis=0)
    for c in range(q_ref.shape[1]):
        for i in (range(rows // BLOCK) if blocks is None else blocks):
            r = slice(i * BLOCK, (i + 1) * BLOCK)
            q = q_ref[0, c, r, :]
            if i == 0:
                k_prev, v_prev = kp_ref[0, c], vp_ref[0, c]
            else:
                before = slice((i - 1) * BLOCK, i * BLOCK)
                k_prev, v_prev = ko_ref[0, c, before, :], vo_ref[0, c, before, :]
            if zero_row0:
                k_prev, v_prev = zero_first(k_prev), zero_first(v_prev)
            k = jnp.concatenate([k_prev, ko_ref[0, c, r, :]], axis=0)
            v = jnp.concatenate([v_prev, vo_ref[0, c, r, :]], axis=0)
            for p in range(groups):
                cols = slice(p * LANES, (p + 1) * LANES)
                v_ones = jnp.concatenate([v[:, cols], ones], axis=1)
                bias = bias_0[p] if i == 0 else bias_rest[p]
                in_matmul = isinstance(bias, tuple)
                emit(c, i, p, *_pair_attention(q[:, cols], k[:, cols], v_ones,
                                               None if in_matmul else bias,
                                               bias if in_matmul else None))
                yield


def _qkv_specs(dil, rows, width):
    n = rows // dil
    own = pl.BlockSpec((1, dil, n, width), lambda bi, j: (bi, 0, j, 0))
    prev = pl.BlockSpec(
        (1, dil, BLOCK, width), lambda bi, j: (bi, 0, jnp.maximum(j * (n // BLOCK) - 1, 0), 0))
    return [own, prev, own, prev, own]


def _attn_a_out_kernel(sink_ref, q_ref, kp_ref, ko_ref, vp_ref, vo_ref, gate_ref,
                       yb_ref, yc_ref, x_ref, w_ref, g_ref, out_ref, wb_ref):
    @pl.when((pl.program_id(0) == 0) & (pl.program_id(1) == 0))
    def _():
        wb_ref[...] = w_ref[...].astype(jnp.bfloat16)

    groups = q_ref.shape[3] // LANES
    bias_0, bias_rest = _band_biases(A_WINDOW - 1, pl.program_id(1) == 0, sink_ref, groups)
    tiles = {}

    def emit(c, i, p, acc, l, m):
        r = slice(i * BLOCK, (i + 1) * BLOCK)
        cols = slice(p * LANES, (p + 1) * LANES)
        gate = gate_ref[0, r, cols].astype(jnp.float32)
        tiles[i, p] = (acc * (1.0 / l) * gate).astype(jnp.bfloat16)

    per_group = OUT_GROUP // BLOCK
    for grp in range(q_ref.shape[2] // OUT_GROUP):
        blocks = range(grp * per_group, (grp + 1) * per_group)
        _band_chains(q_ref, kp_ref, ko_ref, vp_ref, vo_ref, bias_0, bias_rest, True, emit,
                     blocks=blocks)
        r = slice(grp * OUT_GROUP, (grp + 1) * OUT_GROUP)
        ya = jnp.concatenate(
            [jnp.concatenate([tiles[i, p] for p in range(groups)], axis=1) for i in blocks], axis=0)
        y = jnp.concatenate([ya, yb_ref[0, r, :], yc_ref[0, r, :]], axis=1)
        z = jnp.dot(y, wb_ref[...], preferred_element_type=jnp.float32)
        out_ref[0, r, :] = x_ref[0, r, :] + z * _rms_scale(z) * g_ref[...]


def _attn_a_out(q, k, v, gate, sink, yb, yc, x, w, g):
    b, _, s, wq = q.shape
    d = x.shape[2]
    rows = OUT_ROWS
    row = lambda width: pl.BlockSpec((1, rows, width), lambda bi, j: (bi, j, 0))
    return pl.pallas_call(
        _attn_a_out_kernel,
        grid=(b, s // rows),
        in_specs=[pl.BlockSpec(memory_space=pltpu.SMEM)] + _qkv_specs(1, rows, wq)
        + [row(wq), row(yb.shape[2]), row(yc.shape[2]), row(d),
           pl.BlockSpec((d, d), lambda bi, j: (0, 0), pipeline_mode=pl.Buffered(1)),
           pl.BlockSpec((1, d), lambda bi, j: (0, 0))],
        out_specs=row(d),
        out_shape=jax.ShapeDtypeStruct((b, s, d), x.dtype),
        scratch_shapes=[pltpu.VMEM((d, d), jnp.bfloat16)],
        compiler_params=pltpu.CompilerParams(
            dimension_semantics=("arbitrary", "arbitrary"), vmem_limit_bytes=VMEM_LIMIT),
        name="attn_a_out",
    )(sink, q, k, k, v, v, gate, yb, yc, x, w, g)


def _stage_pitch(dil):
    pitch = dil if dil % 8 else dil + dil // 2
    assert pitch == dil or pitch % 8 == 0
    return pitch


def _stage_arrays(dil):
    return 3 if _stage_pitch(dil) == dil else 2


def _stage_sets(level):
    return 1 if level == 0 else 2


def _attn_b_kernel(*refs):
    n_cfg = len(B_CONFIGS)
    gate_ref, y_ref = refs[5 * n_cfg:5 * n_cfg + 2]
    stage = list(refs[5 * n_cfg + 2:])
    span = y_ref.shape[1]
    first_step = pl.program_id(1) == 0
    groups = B_W // LANES
    order = sorted(range(n_cfg), key=lambda c: -B_CONFIGS[c][1])
    assert B_CONFIGS[order[-1]][1] == 1
    qkv, biases, cover = {}, {}, {}
    for c in order:
        win, dil = B_CONFIGS[c]
        qkv[dil] = refs[5 * c:5 * c + 5]
        biases[dil] = _band_mask_keys(win // dil, first_step, groups)
        cover[dil] = dil * BLOCK
    dils = [B_CONFIGS[c][1] for c in order]
    assert len(dils) == 3
    staged = {dil: [tuple(stage.pop(0) for _ in range(_stage_arrays(dil)))
                    for _ in range(_stage_sets(level))]
              for level, dil in enumerate(dils[:-1])}

    def steps(level, lo):
        dil = dils[level]
        blk = lo // cover[dil]

        def stage_of(coarse):
            sets = staged[coarse]
            return sets[(lo // cover[coarse]) % len(sets)]

        def emit_staged(cls, i, p, acc, l, m):
            rows = pl.ds(cls, BLOCK, stride=_stage_pitch(dil))
            vals = (acc, l, m)
            if _stage_arrays(dil) == 2:
                vals = (acc * (1.0 / l), m + jnp.log(l) * LOG2E)
            for ref, val in zip(stage_of(dil), vals):
                ref[p, rows, :] = val

        def staged_rows(ref, coarse, p):
            pitch = _stage_pitch(coarse)
            run0 = (lo % cover[coarse]) // coarse
            if pitch == coarse:
                return ref[p, run0 * pitch:run0 * pitch + BLOCK, :]
            return jnp.concatenate(
                [ref[p, (run0 + g) * pitch:(run0 + g) * pitch + coarse, :]
                 for g in range(BLOCK // coarse)], axis=0)

        def emit_mixed(cls, i, p, acc, l, m):
            cols = slice(p * LANES, (p + 1) * LANES)
            accs, ls, ms = [acc], [l], [m]
            for coarse in dils[:-1]:
                vals = [staged_rows(ref, coarse, p) for ref in stage_of(coarse)]
                accs.append(vals[0])
                ls.append(vals[1] if len(vals) == 3 else None)
                ms.append(vals[-1])
            mx = functools.reduce(jnp.maximum, ms)
            wts = [jnp.exp2(mi - mx) for mi in ms]
            num = sum(w * a for w, a in zip(wts[1:], accs[1:])) + wts[0] * accs[0]
            den = sum((w if li is None else w * li) for w, li in zip(wts[1:], ls[1:])) + wts[0] * ls[0]
            r = slice(lo, lo + BLOCK)
            gate = gate_ref[0, r, cols].astype(jnp.float32)
            y_ref[0, r, cols] = (num * (1.0 / den) * gate).astype(y_ref.dtype)

        last = level == len(dils) - 1
        return _band_chain_steps(*qkv[dil], *biases[dil], False,
                                 emit_mixed if last else emit_staged, blocks=(blk,))

    coarse, middle, fine = dils
    for lo0 in range(0, span, cover[coarse]):
        for _ in steps(0, lo0):
            pass
        subs = list(range(lo0, lo0 + cover[coarse], cover[middle]))
        for _ in steps(1, subs[0]):
            pass
        for n, sub in enumerate(subs):
            mixing = itertools.chain.from_iterable(
                steps(2, lo) for lo in range(sub, sub + cover[middle], cover[fine]))
            ahead = steps(1, subs[n + 1]) if n + 1 < len(subs) else ()
            _interleave(mixing, ahead)


def _attn_b(qkv_b, gate):
    n_cfg = len(B_CONFIGS)
    b, _, s, wq = qkv_b[0].shape
    rows = ATTN_ROWS
    in_specs, args = [], []
    for c, (_, dil) in enumerate(B_CONFIGS):
        q, k, v = qkv_b[c], qkv_b[n_cfg + c], qkv_b[2 * n_cfg + c]
        in_specs += _qkv_specs(dil, rows, wq)
        args += [q, k, k, v, v]
    row = pl.BlockSpec((1, rows, wq), lambda bi, j: (bi, j, 0))
    stages = []
    for level, dil in enumerate(sorted((d for _, d in B_CONFIGS if d > 1), reverse=True)):
        assert rows % (dil * BLOCK) == 0
        stages += [pltpu.VMEM((wq // LANES, _stage_pitch(dil) * BLOCK, LANES),
                              jnp.float32)] * (_stage_arrays(dil) * _stage_sets(level))
    return pl.pallas_call(
        _attn_b_kernel,
        grid=(b, s // rows),
        in_specs=in_specs + [row],
        out_specs=row,
        out_shape=jax.ShapeDtypeStruct((b, s, wq), jnp.bfloat16),
        scratch_shapes=stages,
        compiler_params=pltpu.CompilerParams(
            dimension_semantics=("parallel", "parallel"), vmem_limit_bytes=ATTN_B_VMEM_LIMIT),
        name="attn_b",
    )(*args, gate)


def _rope_tables(seq):
    inv_freq = ROPE_THETA ** (-np.arange(0, ROT_DIM, 2, dtype=np.float64) / ROT_DIM)
    ang = np.arange(seq, dtype=np.float64)[:, None] * inv_freq[None, :]
    cos, sin = np.cos(ang), np.sin(ang)
    pad = HEAD_DIM - ROT_DIM
    cos_h = np.concatenate([cos, cos, np.ones((seq, pad))], axis=1)
    sin_h = np.concatenate([-sin, sin, np.zeros((seq, pad))], axis=1)
    reps = LANES // HEAD_DIM
    return (jnp.asarray(np.tile(cos_h, (1, reps)), jnp.float32),
            jnp.asarray(np.tile(sin_h, (1, reps)), jnp.float32))


def kernel(x, mem, pre_norm, w_in, sink_a, mem_norm, w_mem_kv, w_out, post_norm):
    depth = pre_norm.shape[0]
    cos_t, sin_t = _rope_tables(x.shape[1])
    h = x
    for l in range(depth):
        mk, mv = _mem_kv(mem, mem_norm[l][None], w_mem_kv[l])
        qa, ka, va, ga, gb, yc, *qkv_b = _in_proj(
            h, pre_norm[l][None], w_in[l], cos_t, sin_t, mk, mv)
        yb = _attn_b(qkv_b, gb)
        h = _attn_a_out(qa[:, None], ka[:, None], va[:, None], ga, sink_a[l], yb, yc, h,
                        w_out[l], post_norm[l][None])
    return h
```

```python
import functools
import itertools
import math

import jax
import jax.numpy as jnp
import numpy as np
from jax import lax
from jax.experimental import pallas as pl
from jax.experimental.pallas import tpu as pltpu

HEAD_DIM = 64
ROT_DIM = HEAD_DIM // 4
ROPE_THETA = 500000.0
BLOCK = 128
C_HEADS = 4
A_Q_HEADS = 6
A_KV_HEADS = 2
A_GROUP = A_Q_HEADS // A_KV_HEADS
A_WINDOW = 128
B_HEADS = 6
B_CONFIGS = ((128, 1), (512, 4), (2048, 16))
RMS_EPS = 1e-6

LANES = 128
A_W = A_Q_HEADS * HEAD_DIM
A_KV_W = A_KV_HEADS * HEAD_DIM
B_W = B_HEADS * HEAD_DIM
C_W = C_HEADS * HEAD_DIM

IN_ROWS = 1024
IN_PARTS = 1
ATTN_ROWS = 2048
OUT_ROWS = 1024
OUT_GROUP = 1024
VMEM_LIMIT = 48 * 1024 * 1024
ATTN_B_VMEM_LIMIT = 56 * 1024 * 1024
IN_VMEM_LIMIT = 60 * 1024 * 1024

MASKED = -3e38
LOG2E = math.log2(math.e)
Q_SCALE = HEAD_DIM ** -0.5 * LOG2E


def _rms_scale(xf):
    return lax.rsqrt(jnp.mean(xf * xf, axis=-1, keepdims=True) + RMS_EPS)


def _silu(g):
    return g * (1.0 / (1.0 + jnp.exp(-g)))


def _left_half():
    return lax.broadcasted_iota(jnp.int32, (1, LANES), 1) < HEAD_DIM


def _pair_attention(q2, kg, v_ones, bias, bias_keys=None):
    rows = q2.shape[0]
    left = _left_half()
    keep_l = jnp.where(left, 1.0, 0.0).astype(q2.dtype)
    qs = jnp.concatenate([q2 * keep_l, q2 * (1 - keep_l)], axis=0)
    if bias_keys is not None:
        mask_t, onehot = bias_keys
        qs = jnp.concatenate([qs, onehot], axis=1)
        kg = jnp.concatenate([kg, mask_t], axis=1)
    s = lax.dot_general(qs, kg, (((1,), (1,)), ((), ())), preferred_element_type=jnp.float32)
    if bias is not None:
        s = s + bias
    m = jnp.max(s, axis=-1, keepdims=True)
    e = jnp.exp2((s - m).astype(jnp.bfloat16))
    pv = jnp.dot(e, v_ones, preferred_element_type=jnp.float32)
    acc = jnp.where(left, pv[:rows, :LANES], pv[rows:, :LANES])
    l = jnp.where(left, pv[:rows, LANES:], pv[rows:, LANES:])
    m_own = jnp.where(left, m[:rows], m[rows:])
    return acc, l, m_own


def _mem_kv_kernel(mem_ref, g_ref, w_ref, mk_ref, mv_ref):
    b, n_mem, d = mem_ref.shape
    m = mem_ref[...].reshape(b * n_mem, d)
    u = (m * _rms_scale(m) * g_ref[...]).astype(jnp.bfloat16)
    kv = jnp.dot(u, w_ref[...].astype(jnp.bfloat16), preferred_element_type=jnp.float32)
    mk_ref[...] = kv[:, :C_W].astype(jnp.bfloat16).reshape(b, n_mem, C_W)
    mv_ref[...] = kv[:, C_W:].astype(jnp.bfloat16).reshape(b, n_mem, C_W)


def _mem_kv(mem, g, w):
    b, n_mem, d = mem.shape
    whole = lambda shape: pl.BlockSpec(shape, lambda i: (0,) * len(shape))
    return pl.pallas_call(
        _mem_kv_kernel,
        grid=(1,),
        in_specs=[whole((b, n_mem, d)), whole((1, d)), whole((d, 2 * C_W))],
        out_specs=[whole((b, n_mem, C_W)), whole((b, n_mem, C_W))],
        out_shape=[jax.ShapeDtypeStruct((b, n_mem, C_W), jnp.bfloat16)] * 2,
        name="mem_kv",
    )(mem, g, w)


_IN_WIDTHS = (("qa", A_W), ("ka", A_KV_W), ("va", A_KV_W), ("ga", A_W),
              ("qb", B_W), ("kb", B_W), ("vb", B_W), ("gb", B_W), ("qc", C_W), ("gc", C_W))
D_IN = sum(w for _, w in _IN_WIDTHS)
_DOT_GROUPS = (("qa", "ka"), ("va", "ga"), ("qb", "kb"), ("vb", "gb"), ("qc", "gc"))


def _rope(t, cos_t, sin_t, first8):
    half = ROT_DIM // 2
    out = []
    for j in range(t.shape[1] // LANES):
        tj = t[:, j * LANES:(j + 1) * LANES]
        partner = jnp.where(first8, pltpu.roll(tj, LANES - half, 1), pltpu.roll(tj, half, 1))
        out.append(tj * cos_t + partner * sin_t)
    return out[0] if len(out) == 1 else jnp.concatenate(out, axis=1)


def _expand_kv(t):
    left = _left_half()
    swapped = pltpu.roll(t, HEAD_DIM, 1)
    groups = []
    for p in range(A_Q_HEADS // 2):
        kv_l, kv_r = (2 * p) // A_GROUP, (2 * p + 1) // A_GROUP
        if (kv_l, kv_r) == (0, 1):
            groups.append(t)
        elif kv_l == kv_r == 0:
            groups.append(jnp.where(left, t, swapped))
        else:
            groups.append(jnp.where(left, swapped, t))
    return jnp.concatenate(groups, axis=1)


def _store_dilated(t, refs, part, stage_ref, stage4_ref):
    rows = t.shape[0]
    slabs = t.shape[1] // LANES
    refs[1][0, 0, part * rows:(part + 1) * rows, :] = t.astype(refs[1].dtype)
    for sl in range(slabs):
        stage_ref[sl] = t[:, sl * LANES:(sl + 1) * LANES]
    n4, n16 = rows // 4, rows // 16
    for sl in range(slabs):
        cols = slice(sl * LANES, (sl + 1) * LANES)
        for r in range(4):
            c4 = stage_ref[sl, pl.ds(r, n4, stride=4), :]
            refs[4][0, r, part * n4:(part + 1) * n4, cols] = c4.astype(refs[4].dtype)
            stage4_ref[sl, r * n4:(r + 1) * n4, :] = c4
        for r in range(4):
            for q in range(4):
                c16 = stage4_ref[sl, pl.ds(r * n4 + q, n16, stride=4), :]
                refs[16][0, r + 4 * q, part * n16:(part + 1) * n16, cols] = c16.astype(
                    refs[16].dtype)


def _in_proj_kernel(x_ref, g_ref, w_ref, cos_ref, sin_ref, mk_ref, mv_ref,
                    qa_ref, ka_ref, va_ref, ga_ref, gb_ref, yc_ref, *rest):
    dils = [d for _, d in B_CONFIGS]
    qb_refs = dict(zip(dils, rest[0:3]))
    kb_refs = dict(zip(dils, rest[3:6]))
    vb_refs = dict(zip(dils, rest[6:9]))
    wb_ref = rest[9]
    stages = rest[10:]
    bf16 = jnp.bfloat16

    @pl.when((pl.program_id(0) == 0) & (pl.program_id(1) == 0))
    def _():
        for c0 in range(0, D_IN, 4 * LANES):
            wb_ref[:, c0:c0 + 4 * LANES] = w_ref[:, c0:c0 + 4 * LANES].astype(bf16)

    lane = lax.broadcasted_iota(jnp.int32, (1, LANES), 1)
    first8 = (lane % HEAD_DIM) < (ROT_DIM // 2)
    offsets = {}
    off = 0
    for name, wd in _IN_WIDTHS:
        offsets[name] = (off, wd)
        off += wd
    ones = jnp.ones((mv_ref.shape[1], LANES), bf16)

    part_rows = x_ref.shape[1] // IN_PARTS
    for part in range(IN_PARTS):
        r = slice(part * part_rows, (part + 1) * part_rows)
        stage_ref, stage4_ref = stages[2 * part:2 * part + 2]
        xf = x_ref[0, r, :]
        u = (xf * _rms_scale(xf) * g_ref[...]).astype(bf16)
        cos_t = cos_ref[r, :]
        sin_t = sin_ref[r, :]
        rope = lambda t, cos_t=cos_t, sin_t=sin_t: _rope(t, cos_t, sin_t, first8)

        def proj(names, u=u):
            c0 = offsets[names[0]][0]
            c1 = offsets[names[-1]][0] + offsets[names[-1]][1]
            full = jnp.dot(u, wb_ref[:, c0:c1], preferred_element_type=jnp.float32)
            return [full[:, offsets[n][0] - c0:offsets[n][0] - c0 + offsets[n][1]] for n in names]

        qc, gc = proj(_DOT_GROUPS[4])
        qc = (qc * Q_SCALE).astype(bf16)
        oc = []
        for p in range(C_W // LANES):
            cols = slice(p * LANES, (p + 1) * LANES)
            v_ones = jnp.concatenate([mv_ref[0, :, cols], ones], axis=1)
            acc, l, _ = _pair_attention(qc[:, cols], mk_ref[0, :, cols], v_ones, None)
            oc.append(acc * (1.0 / l))
        yc_ref[0, r, :] = (jnp.concatenate(oc, axis=1) * _silu(gc)).astype(bf16)

        qb, kb = proj(_DOT_GROUPS[2])
        _store_dilated(rope(qb) * Q_SCALE, qb_refs, part, stage_ref, stage4_ref)
        _store_dilated(rope(kb), kb_refs, part, stage_ref, stage4_ref)
        vb, gb = proj(_DOT_GROUPS[3])
        _store_dilated(vb, vb_refs, part, stage_ref, stage4_ref)
        gb_ref[0, r, :] = _silu(gb).astype(bf16)
        qa, ka = proj(_DOT_GROUPS[0])
        qa_ref[0, r, :] = (rope(qa) * Q_SCALE).astype(bf16)
        ka_ref[0, r, :] = _expand_kv(rope(ka)).astype(bf16)
        va, ga = proj(_DOT_GROUPS[1])
        va_ref[0, r, :] = _expand_kv(va).astype(bf16)
        ga_ref[0, r, :] = _silu(ga).astype(bf16)


def _in_proj(x, g, w, cos_t, sin_t, mk, mv):
    b, s, d = x.shape
    tm = IN_ROWS
    n_mem = mk.shape[1]
    row = lambda width: pl.BlockSpec((1, tm, width), lambda i, bi: (bi, i, 0))
    out_widths = (A_W, A_W, A_W, A_W, B_W, C_W)
    out_specs = [row(wd) for wd in out_widths]
    out_shape = [jax.ShapeDtypeStruct((b, s, wd), jnp.bfloat16) for wd in out_widths]
    for _ in range(3):
        for _, dil in B_CONFIGS:
            out_specs.append(pl.BlockSpec((1, dil, tm // dil, B_W), lambda i, bi: (bi, 0, i, 0)))
            out_shape.append(jax.ShapeDtypeStruct((b, dil, s // dil, B_W), jnp.bfloat16))
    stage = pltpu.VMEM((B_W // LANES, tm // IN_PARTS, LANES), jnp.float32)
    return pl.pallas_call(
        _in_proj_kernel,
        grid=(s // tm, b),
        in_specs=[
            row(d),
            pl.BlockSpec((1, d), lambda i, bi: (0, 0)),
            pl.BlockSpec((d, D_IN), lambda i, bi: (0, 0), pipeline_mode=pl.Buffered(1)),
            pl.BlockSpec((tm, LANES), lambda i, bi: (i, 0)),
            pl.BlockSpec((tm, LANES), lambda i, bi: (i, 0)),
            pl.BlockSpec((1, n_mem, C_W), lambda i, bi: (bi, 0, 0)),
            pl.BlockSpec((1, n_mem, C_W), lambda i, bi: (bi, 0, 0)),
        ],
        out_specs=out_specs,
        out_shape=out_shape,
        scratch_shapes=[pltpu.VMEM((d, D_IN), jnp.bfloat16)] + [stage] * (2 * IN_PARTS),
        compiler_params=pltpu.CompilerParams(
            dimension_semantics=("arbitrary", "arbitrary"), vmem_limit_bytes=IN_VMEM_LIMIT),
        name="in_proj",
    )(x, g, w, cos_t, sin_t, mk, mv)


def _band_biases(max_dist, first_step, sink_ref, groups):
    qi = lax.broadcasted_iota(jnp.int32, (BLOCK, 2 * BLOCK), 0)
    kc = lax.broadcasted_iota(jnp.int32, (BLOCK, 2 * BLOCK), 1)
    dist = qi + BLOCK - kc
    band = (dist >= 0) & (dist <= max_dist)
    first_key = jnp.where(first_step, BLOCK, 0)
    neg = jnp.float32(-jnp.inf)
    bias_any = jnp.where(band, 0.0, neg)
    bias_first = jnp.where(band & (kc >= first_key), 0.0, neg)

    def pair_bias(base, p):
        if sink_ref is None:
            return jnp.concatenate([base, base], axis=0)
        assert max_dist < BLOCK
        halves = [jnp.where(kc == 0, sink_ref[2 * p + h] * LOG2E, base) for h in range(2)]
        return jnp.concatenate(halves, axis=0)

    if sink_ref is None:
        first, rest = pair_bias(bias_first, 0), pair_bias(bias_any, 0)
        return [first] * groups, [rest] * groups
    return ([pair_bias(bias_first, p) for p in range(groups)],
            [pair_bias(bias_any, p) for p in range(groups)])


def _band_mask_keys(max_dist, first_step, groups):
    kc = lax.broadcasted_iota(jnp.int32, (2 * BLOCK, BLOCK), 0)
    qi = lax.broadcasted_iota(jnp.int32, (2 * BLOCK, BLOCK), 1)
    dist = qi + BLOCK - kc
    band = (dist >= 0) & (dist <= max_dist)
    first_key = jnp.where(first_step, BLOCK, 0)
    rest = jnp.where(band, 0.0, MASKED).astype(jnp.bfloat16)
    first = jnp.where(band & (kc >= first_key), 0.0, MASKED).astype(jnp.bfloat16)
    row = lax.broadcasted_iota(jnp.int32, (2 * BLOCK, BLOCK), 0)
    onehot = jnp.where(row % BLOCK == qi, 1.0, 0.0).astype(jnp.bfloat16)
    return [(first, onehot)] * groups, [(rest, onehot)] * groups


def _band_chains(*args, **kwargs):
    for _ in _band_chain_steps(*args, **kwargs):
        pass


def _interleave(*step_iters):
    iters = [iter(s) for s in step_iters]
    while iters:
        iters = [it for it in iters if next(it, StopIteration) is not StopIteration]


def _band_chain_steps(q_ref, kp_ref, ko_ref, vp_ref, vo_ref, bias_0, bias_rest, zero_row0, emit,
                      blocks=None):
    rows = q_ref.shape[2]
    groups = q_ref.shape[3] // LANES
    ones = jnp.ones((2 * BLOCK, LANES), jnp.bfloat16)
    tile = 16
    row0 = lax.broadcasted_iota(jnp.int32, (tile, 1), 0) == 0
    zero_first = lambda t: jnp.concatenate(
        [jnp.where(row0, jnp.zeros_like(t[:tile]), t[:tile]), t[tile:]], axis=0)
    for c in range(q_ref.shape[1]):
        for i in (range(rows // BLOCK) if blocks is None else blocks):
            r = slice(i * BLOCK, (i + 1) * BLOCK)
            q = q_ref[0, c, r, :]
            if i == 0:
                k_prev, v_prev = kp_ref[0, c], vp_ref[0, c]
            else:
                before = slice((i - 1) * BLOCK, i * BLOCK)
                k_prev, v_prev = ko_ref[0, c, before, :], vo_ref[0, c, before, :]
            if zero_row0:
                k_prev, v_prev = zero_first(k_prev), zero_first(v_prev)
            k = jnp.concatenate([k_prev, ko_ref[0, c, r, :]], axis=0)
            v = jnp.concatenate([v_prev, vo_ref[0, c, r, :]], axis=0)
            for p in range(groups):
                cols = slice(p * LANES, (p + 1) * LANES)
                v_ones = jnp.concatenate([v[:, cols], ones], axis=1)
                bias = bias_0[p] if i == 0 else bias_rest[p]
                in_matmul = isinstance(bias, tuple)
                emit(c, i, p, *_pair_attention(q[:, cols], k[:, cols], v_ones,
                                               None if in_matmul else bias,
                                               bias if in_matmul else None))
                yield


def _qkv_specs(dil, rows, width):
    n = rows // dil
    own = pl.BlockSpec((1, dil, n, width), lambda bi, j: (bi, 0, j, 0))
    prev = pl.BlockSpec(
        (1, dil, BLOCK, width), lambda bi, j: (bi, 0, jnp.maximum(j * (n // BLOCK) - 1, 0), 0))
    return [own, prev, own, prev, own]


def _attn_a_out_kernel(sink_ref, q_ref, kp_ref, ko_ref, vp_ref, vo_ref, gate_ref,
                       yb_ref, yc_ref, x_ref, w_ref, g_ref, out_ref, wb_ref):
    @pl.when((pl.program_id(0) == 0) & (pl.program_id(1) == 0))
    def _():
        wb_ref[...] = w_ref[...].astype(jnp.bfloat16)

    groups = q_ref.shape[3] // LANES
    bias_0, bias_rest = _band_biases(A_WINDOW - 1, pl.program_id(1) == 0, sink_ref, groups)
    tiles = {}

    def emit(c, i, p, acc, l, m):
        r = slice(i * BLOCK, (i + 1) * BLOCK)
        cols = slice(p * LANES, (p + 1) * LANES)
        gate = gate_ref[0, r, cols].astype(jnp.float32)
        tiles[i, p] = (acc * (1.0 / l) * gate).astype(jnp.bfloat16)

    per_group = OUT_GROUP // BLOCK
    for grp in range(q_ref.shape[2] // OUT_GROUP):
        blocks = range(grp * per_group, (grp + 1) * per_group)
        _band_chains(q_ref, kp_ref, ko_ref, vp_ref, vo_ref, bias_0, bias_rest, True, emit,
                     blocks=blocks)
        r = slice(grp * OUT_GROUP, (grp + 1) * OUT_GROUP)
        ya = jnp.concatenate(
            [jnp.concatenate([tiles[i, p] for p in range(groups)], axis=1) for i in blocks], axis=0)
        y = jnp.concatenate([ya, yb_ref[0, r, :], yc_ref[0, r, :]], axis=1)
        z = jnp.dot(y, wb_ref[...], preferred_element_type=jnp.float32)
        out_ref[0, r, :] = x_ref[0, r, :] + z * _rms_scale(z) * g_ref[...]


def _attn_a_out(q, k, v, gate, sink, yb, yc, x, w, g):
    b, _, s, wq = q.shape
    d = x.shape[2]
    rows = OUT_ROWS
    row = lambda width: pl.BlockSpec((1, rows, width), lambda bi, j: (bi, j, 0))
    return pl.pallas_call(
        _attn_a_out_kernel,
        grid=(b, s // rows),
        in_specs=[pl.BlockSpec(memory_space=pltpu.SMEM)] + _qkv_specs(1, rows, wq)
        + [row(wq), row(yb.shape[2]), row(yc.shape[2]), row(d),
           pl.BlockSpec((d, d), lambda bi, j: (0, 0), pipeline_mode=pl.Buffered(1)),
           pl.BlockSpec((1, d), lambda bi, j: (0, 0))],
        out_specs=row(d),
        out_shape=jax.ShapeDtypeStruct((b, s, d), x.dtype),
        scratch_shapes=[pltpu.VMEM((d, d), jnp.bfloat16)],
        compiler_params=pltpu.CompilerParams(
            dimension_semantics=("arbitrary", "arbitrary"), vmem_limit_bytes=VMEM_LIMIT),
        name="attn_a_out",
    )(sink, q, k, k, v, v, gate, yb, yc, x, w, g)


def _stage_pitch(dil):
    pitch = dil if dil % 8 else dil + dil // 2
    assert pitch == dil or pitch % 8 == 0
    return pitch


def _stage_arrays(dil):
    return 3 if _stage_pitch(dil) == dil else 2


def _stage_sets(level):
    return 1 if level == 0 else 2


def _attn_b_kernel(*refs):
    n_cfg = len(B_CONFIGS)
    gate_ref, y_ref = refs[5 * n_cfg:5 * n_cfg + 2]
    stage = list(refs[5 * n_cfg + 2:])
    span = y_ref.shape[1]
    first_step = pl.program_id(1) == 0
    groups = B_W // LANES
    order = sorted(range(n_cfg), key=lambda c: -B_CONFIGS[c][1])
    assert B_CONFIGS[order[-1]][1] == 1
    qkv, biases, cover = {}, {}, {}
    for c in order:
        win, dil = B_CONFIGS[c]
        qkv[dil] = refs[5 * c:5 * c + 5]
        biases[dil] = _band_mask_keys(win // dil, first_step, groups)
        cover[dil] = dil * BLOCK
    dils = [B_CONFIGS[c][1] for c in order]
    assert len(dils) == 3
    staged = {dil: [tuple(stage.pop(0) for _ in range(_stage_arrays(dil)))
                    for _ in range(_stage_sets(level))]
              for level, dil in enumerate(dils[:-1])}

    def steps(level, lo):
        dil = dils[level]
        blk = lo // cover[dil]

        def stage_of(coarse):
            sets = staged[coarse]
            return sets[(lo // cover[coarse]) % len(sets)]

        def emit_staged(cls, i, p, acc, l, m):
            rows = pl.ds(cls, BLOCK, stride=_stage_pitch(dil))
            vals = (acc, l, m)
            if _stage_arrays(dil) == 2:
                vals = (acc * (1.0 / l), m + jnp.log(l) * LOG2E)
            for ref, val in zip(stage_of(dil), vals):
                ref[p, rows, :] = val

        def staged_rows(ref, coarse, p):
            pitch = _stage_pitch(coarse)
            run0 = (lo % cover[coarse]) // coarse
            if pitch == coarse:
                return ref[p, run0 * pitch:run0 * pitch + BLOCK, :]
            return jnp.concatenate(
                [ref[p, (run0 + g) * pitch:(run0 + g) * pitch + coarse, :]
                 for g in range(BLOCK // coarse)], axis=0)

        def emit_mixed(cls, i, p, acc, l, m):
            cols = slice(p * LANES, (p + 1) * LANES)
            accs, ls, ms = [acc], [l], [m]
            for coarse in dils[:-1]:
                vals = [staged_rows(ref, coarse, p) for ref in stage_of(coarse)]
                accs.append(vals[0])
                ls.append(vals[1] if len(vals) == 3 else None)
                ms.append(vals[-1])
            mx = functools.reduce(jnp.maximum, ms)
            wts = [jnp.exp2(mi - mx) for mi in ms]
            num = sum(w * a for w, a in zip(wts[1:], accs[1:])) + wts[0] * accs[0]
            den = sum((w if li is None else w * li) for w, li in zip(wts[1:], ls[1:])) + wts[0] * ls[0]
            r = slice(lo, lo + BLOCK)
            gate = gate_ref[0, r, cols].astype(jnp.float32)
            y_ref[0, r, cols] = (num * (1.0 / den) * gate).astype(y_ref.dtype)

        last = level == len(dils) - 1
        return _band_chain_steps(*qkv[dil], *biases[dil], False,
                                 emit_mixed if last else emit_staged, blocks=(blk,))

    coarse, middle, fine = dils
    for lo0 in range(0, span, cover[coarse]):
        for _ in steps(0, lo0):
            pass
        subs = list(range(lo0, lo0 + cover[coarse], cover[middle]))
        for _ in steps(1, subs[0]):
            pass
        for n, sub in enumerate(subs):
            mixing = itertools.chain.from_iterable(
                steps(2, lo) for lo in range(sub, sub + cover[middle], cover[fine]))
            ahead = steps(1, subs[n + 1]) if n + 1 < len(subs) else ()
            _interleave(mixing, ahead)


def _attn_b(qkv_b, gate):
    n_cfg = len(B_CONFIGS)
    b, _, s, wq = qkv_b[0].shape
    rows = ATTN_ROWS
    in_specs, args = [], []
    for c, (_, dil) in enumerate(B_CONFIGS):
        q, k, v = qkv_b[c], qkv_b[n_cfg + c], qkv_b[2 * n_cfg + c]
        in_specs += _qkv_specs(dil, rows, wq)
        args += [q, k, k, v, v]
    row = pl.BlockSpec((1, rows, wq), lambda bi, j: (bi, j, 0))
    stages = []
    for level, dil in enumerate(sorted((d for _, d in B_CONFIGS if d > 1), reverse=True)):
        assert rows % (dil * BLOCK) == 0
        stages += [pltpu.VMEM((wq // LANES, _stage_pitch(dil) * BLOCK, LANES),
                              jnp.float32)] * (_stage_arrays(dil) * _stage_sets(level))
    return pl.pallas_call(
        _attn_b_kernel,
        grid=(b, s // rows),
        in_specs=in_specs + [row],
        out_specs=row,
        out_shape=jax.ShapeDtypeStruct((b, s, wq), jnp.bfloat16),
        scratch_shapes=stages,
        compiler_params=pltpu.CompilerParams(
            dimension_semantics=("parallel", "parallel"), vmem_limit_bytes=ATTN_B_VMEM_LIMIT),
        name="attn_b",
    )(*args, gate)


def _rope_tables(seq):
    inv_freq = ROPE_THETA ** (-np.arange(0, ROT_DIM, 2, dtype=np.float64) / ROT_DIM)
    ang = np.arange(seq, dtype=np.float64)[:, None] * inv_freq[None, :]
    cos, sin = np.cos(ang), np.sin(ang)
    pad = HEAD_DIM - ROT_DIM
    cos_h = np.concatenate([cos, cos, np.ones((seq, pad))], axis=1)
    sin_h = np.concatenate([-sin, sin, np.zeros((seq, pad))], axis=1)
    reps = LANES // HEAD_DIM
    return (jnp.asarray(np.tile(cos_h, (1, reps)), jnp.float32),
            jnp.asarray(np.tile(sin_h, (1, reps)), jnp.float32))


def kernel(x, mem, pre_norm, w_in, sink_a, mem_norm, w_mem_kv, w_out, post_norm):
    depth = pre_norm.shape[0]
    cos_t, sin_t = _rope_tables(x.shape[1])
    h = x
    for l in range(depth):
        mk, mv = _mem_kv(mem, mem_norm[l][None], w_mem_kv[l])
        qa, ka, va, ga, gb, yc, *qkv_b = _in_proj(
            h, pre_norm[l][None], w_in[l], cos_t, sin_t, mk, mv)
        yb = _attn_b(qkv_b, gb)
        h = _attn_a_out(qa[:, None], ka[:, None], va[:, None], ga, sink_a[l], yb, yc, h,
                        w_out[l], post_norm[l][None])
    return h
```

```python
import functools
import itertools
import math

import jax
import jax.numpy as jnp
import numpy as np
from jax import lax
from jax.experimental import pallas as pl
from jax.experimental.pallas import tpu as pltpu

HEAD_DIM = 64
ROT_DIM = HEAD_DIM // 4
ROPE_THETA = 500000.0
BLOCK = 128
C_HEADS = 4
A_Q_HEADS = 6
A_KV_HEADS = 2
A_GROUP = A_Q_HEADS // A_KV_HEADS
A_WINDOW = 128
B_HEADS = 6
B_CONFIGS = ((128, 1), (512, 4), (2048, 16))
RMS_EPS = 1e-6

LANES = 128
A_W = A_Q_HEADS * HEAD_DIM
A_KV_W = A_KV_HEADS * HEAD_DIM
B_W = B_HEADS * HEAD_DIM
C_W = C_HEADS * HEAD_DIM

IN_ROWS = 1024
IN_PARTS = 1
ATTN_ROWS = 2048
OUT_ROWS = 1024
OUT_GROUP = 256
VMEM_LIMIT = 48 * 1024 * 1024
ATTN_B_VMEM_LIMIT = 56 * 1024 * 1024
IN_VMEM_LIMIT = 60 * 1024 * 1024

MASKED = -3e38
LOG2E = math.log2(math.e)
Q_SCALE = HEAD_DIM ** -0.5 * LOG2E


def _rms_scale(xf):
    return lax.rsqrt(jnp.mean(xf * xf, axis=-1, keepdims=True) + RMS_EPS)


def _silu(g):
    return g * (1.0 / (1.0 + jnp.exp(-g)))


def _left_half():
    return lax.broadcasted_iota(jnp.int32, (1, LANES), 1) < HEAD_DIM


def _pair_attention(q2, kg, v_ones, bias, bias_keys=None):
    rows = q2.shape[0]
    left = _left_half()
    keep_l = jnp.where(left, 1.0, 0.0).astype(q2.dtype)
    qs = jnp.concatenate([q2 * keep_l, q2 * (1 - keep_l)], axis=0)
    if bias_keys is not None:
        mask_t, onehot = bias_keys
        qs = jnp.concatenate([qs, onehot], axis=1)
        kg = jnp.concatenate([kg, mask_t], axis=1)
    s = lax.dot_general(qs, kg, (((1,), (1,)), ((), ())), preferred_element_type=jnp.float32)
    if bias is not None:
        s = s + bias
    m = jnp.max(s, axis=-1, keepdims=True)
    e = jnp.exp2((s - m).astype(jnp.bfloat16))
    pv = jnp.dot(e, v_ones, preferred_element_type=jnp.float32)
    acc = jnp.where(left, pv[:rows, :LANES], pv[rows:, :LANES])
    l = jnp.where(left, pv[:rows, LANES:], pv[rows:, LANES:])
    m_own = jnp.where(left, m[:rows], m[rows:])
    return acc, l, m_own


def _mem_kv_kernel(mem_ref, g_ref, w_ref, mk_ref, mv_ref):
    b, n_mem, d = mem_ref.shape
    m = mem_ref[...].reshape(b * n_mem, d)
    u = (m * _rms_scale(m) * g_ref[...]).astype(jnp.bfloat16)
    kv = jnp.dot(u, w_ref[...].astype(jnp.bfloat16), preferred_element_type=jnp.float32)
    mk_ref[...] = kv[:, :C_W].astype(jnp.bfloat16).reshape(b, n_mem, C_W)
    mv_ref[...] = kv[:, C_W:].astype(jnp.bfloat16).reshape(b, n_mem, C_W)


def _mem_kv(mem, g, w):
    b, n_mem, d = mem.shape
    whole = lambda shape: pl.BlockSpec(shape, lambda i: (0,) * len(shape))
    return pl.pallas_call(
        _mem_kv_kernel,
        grid=(1,),
        in_specs=[whole((b, n_mem, d)), whole((1, d)), whole((d, 2 * C_W))],
        out_specs=[whole((b, n_mem, C_W)), whole((b, n_mem, C_W))],
        out_shape=[jax.ShapeDtypeStruct((b, n_mem, C_W), jnp.bfloat16)] * 2,
        name="mem_kv",
    )(mem, g, w)


_IN_WIDTHS = (("qa", A_W), ("ka", A_KV_W), ("va", A_KV_W), ("ga", A_W),
              ("qb", B_W), ("kb", B_W), ("vb", B_W), ("gb", B_W), ("qc", C_W), ("gc", C_W))
D_IN = sum(w for _, w in _IN_WIDTHS)
_DOT_GROUPS = (("qa", "ka"), ("va", "ga"), ("qb", "kb"), ("vb", "gb"), ("qc", "gc"))


def _rope(t, cos_t, sin_t, first8):
    half = ROT_DIM // 2
    out = []
    for j in range(t.shape[1] // LANES):
        tj = t[:, j * LANES:(j + 1) * LANES]
        partner = jnp.where(first8, pltpu.roll(tj, LANES - half, 1), pltpu.roll(tj, half, 1))
        out.append(tj * cos_t + partner * sin_t)
    return out[0] if len(out) == 1 else jnp.concatenate(out, axis=1)


def _expand_kv(t):
    left = _left_half()
    swapped = pltpu.roll(t, HEAD_DIM, 1)
    groups = []
    for p in range(A_Q_HEADS // 2):
        kv_l, kv_r = (2 * p) // A_GROUP, (2 * p + 1) // A_GROUP
        if (kv_l, kv_r) == (0, 1):
            groups.append(t)
        elif kv_l == kv_r == 0:
            groups.append(jnp.where(left, t, swapped))
        else:
            groups.append(jnp.where(left, swapped, t))
    return jnp.concatenate(groups, axis=1)


def _store_dilated(t, refs, part, stage_ref, stage4_ref):
    rows = t.shape[0]
    slabs = t.shape[1] // LANES
    refs[1][0, 0, part * rows:(part + 1) * rows, :] = t.astype(refs[1].dtype)
    for sl in range(slabs):
        stage_ref[sl] = t[:, sl * LANES:(sl + 1) * LANES]
    n4, n16 = rows // 4, rows // 16
    for sl in range(slabs):
        cols = slice(sl * LANES, (sl + 1) * LANES)
        for r in range(4):
            c4 = stage_ref[sl, pl.ds(r, n4, stride=4), :]
            refs[4][0, r, part * n4:(part + 1) * n4, cols] = c4.astype(refs[4].dtype)
            stage4_ref[sl, r * n4:(r + 1) * n4, :] = c4
        for r in range(4):
            for q in range(4):
                c16 = stage4_ref[sl, pl.ds(r * n4 + q, n16, stride=4), :]
                refs[16][0, r + 4 * q, part * n16:(part + 1) * n16, cols] = c16.astype(
                    refs[16].dtype)


def _in_proj_kernel(x_ref, g_ref, w_ref, cos_ref, sin_ref, mk_ref, mv_ref,
                    qa_ref, ka_ref, va_ref, ga_ref, gb_ref, yc_ref, *rest):
    dils = [d for _, d in B_CONFIGS]
    qb_refs = dict(zip(dils, rest[0:3]))
    kb_refs = dict(zip(dils, rest[3:6]))
    vb_refs = dict(zip(dils, rest[6:9]))
    wb_ref = rest[9]
    stages = rest[10:]
    bf16 = jnp.bfloat16

    @pl.when((pl.program_id(0) == 0) & (pl.program_id(1) == 0))
    def _():
        for c0 in range(0, D_IN, 4 * LANES):
            wb_ref[:, c0:c0 + 4 * LANES] = w_ref[:, c0:c0 + 4 * LANES].astype(bf16)

    lane = lax.broadcasted_iota(jnp.int32, (1, LANES), 1)
    first8 = (lane % HEAD_DIM) < (ROT_DIM // 2)
    offsets = {}
    off = 0
    for name, wd in _IN_WIDTHS:
        offsets[name] = (off, wd)
        off += wd
    ones = jnp.ones((mv_ref.shape[1], LANES), bf16)

    part_rows = x_ref.shape[1] // IN_PARTS
    for part in range(IN_PARTS):
        r = slice(part * part_rows, (part + 1) * part_rows)
        stage_ref, stage4_ref = stages[2 * part:2 * part + 2]
        xf = x_ref[0, r, :]
        u = (xf * _rms_scale(xf) * g_ref[...]).astype(bf16)
        cos_t = cos_ref[r, :]
        sin_t = sin_ref[r, :]
        rope = lambda t, cos_t=cos_t, sin_t=sin_t: _rope(t, cos_t, sin_t, first8)

        def proj(names, u=u):
            c0 = offsets[names[0]][0]
            c1 = offsets[names[-1]][0] + offsets[names[-1]][1]
            full = jnp.dot(u, wb_ref[:, c0:c1], preferred_element_type=jnp.float32)
            return [full[:, offsets[n][0] - c0:offsets[n][0] - c0 + offsets[n][1]] for n in names]

        qc, gc = proj(_DOT_GROUPS[4])
        qc = (qc * Q_SCALE).astype(bf16)
        oc = []
        for p in range(C_W // LANES):
            cols = slice(p * LANES, (p + 1) * LANES)
            v_ones = jnp.concatenate([mv_ref[0, :, cols], ones], axis=1)
            acc, l, _ = _pair_attention(qc[:, cols], mk_ref[0, :, cols], v_ones, None)
            oc.append(acc * (1.0 / l))
        yc_ref[0, r, :] = (jnp.concatenate(oc, axis=1) * _silu(gc)).astype(bf16)

        qb, kb = proj(_DOT_GROUPS[2])
        _store_dilated(rope(qb) * Q_SCALE, qb_refs, part, stage_ref, stage4_ref)
        _store_dilated(rope(kb), kb_refs, part, stage_ref, stage4_ref)
        vb, gb = proj(_DOT_GROUPS[3])
        _store_dilated(vb, vb_refs, part, stage_ref, stage4_ref)
        gb_ref[0, r, :] = _silu(gb).astype(bf16)
        qa, ka = proj(_DOT_GROUPS[0])
        qa_ref[0, r, :] = (rope(qa) * Q_SCALE).astype(bf16)
        ka_ref[0, r, :] = _expand_kv(rope(ka)).astype(bf16)
        va, ga = proj(_DOT_GROUPS[1])
        va_ref[0, r, :] = _expand_kv(va).astype(bf16)
        ga_ref[0, r, :] = _silu(ga).astype(bf16)


def _in_proj(x, g, w, cos_t, sin_t, mk, mv):
    b, s, d = x.shape
    tm = IN_ROWS
    n_mem = mk.shape[1]
    row = lambda width: pl.BlockSpec((1, tm, width), lambda i, bi: (bi, i, 0))
    out_widths = (A_W, A_W, A_W, A_W, B_W, C_W)
    out_specs = [row(wd) for wd in out_widths]
    out_shape = [jax.ShapeDtypeStruct((b, s, wd), jnp.bfloat16) for wd in out_widths]
    for _ in range(3):
        for _, dil in B_CONFIGS:
            out_specs.append(pl.BlockSpec((1, dil, tm // dil, B_W), lambda i, bi: (bi, 0, i, 0)))
            out_shape.append(jax.ShapeDtypeStruct((b, dil, s // dil, B_W), jnp.bfloat16))
    stage = pltpu.VMEM((B_W // LANES, tm // IN_PARTS, LANES), jnp.float32)
    return pl.pallas_call(
        _in_proj_kernel,
        grid=(s // tm, b),
        in_specs=[
            row(d),
            pl.BlockSpec((1, d), lambda i, bi: (0, 0)),
            pl.BlockSpec((d, D_IN), lambda i, bi: (0, 0), pipeline_mode=pl.Buffered(1)),
            pl.BlockSpec((tm, LANES), lambda i, bi: (i, 0)),
            pl.BlockSpec((tm, LANES), lambda i, bi: (i, 0)),
            pl.BlockSpec((1, n_mem, C_W), lambda i, bi: (bi, 0, 0)),
            pl.BlockSpec((1, n_mem, C_W), lambda i, bi: (bi, 0, 0)),
        ],
        out_specs=out_specs,
        out_shape=out_shape,
        scratch_shapes=[pltpu.VMEM((d, D_IN), jnp.bfloat16)] + [stage] * (2 * IN_PARTS),
        compiler_params=pltpu.CompilerParams(
            dimension_semantics=("arbitrary", "arbitrary"), vmem_limit_bytes=IN_VMEM_LIMIT),
        name="in_proj",
    )(x, g, w, cos_t, sin_t, mk, mv)


def _band_biases(max_dist, first_step, sink_ref, groups):
    qi = lax.broadcasted_iota(jnp.int32, (BLOCK, 2 * BLOCK), 0)
    kc = lax.broadcasted_iota(jnp.int32, (BLOCK, 2 * BLOCK), 1)
    dist = qi + BLOCK - kc
    band = (dist >= 0) & (dist <= max_dist)
    first_key = jnp.where(first_step, BLOCK, 0)
    neg = jnp.float32(-jnp.inf)
    bias_any = jnp.where(band, 0.0, neg)
    bias_first = jnp.where(band & (kc >= first_key), 0.0, neg)

    def pair_bias(base, p):
        if sink_ref is None:
            return jnp.concatenate([base, base], axis=0)
        assert max_dist < BLOCK
        halves = [jnp.where(kc == 0, sink_ref[2 * p + h] * LOG2E, base) for h in range(2)]
        return jnp.concatenate(halves, axis=0)

    if sink_ref is None:
        first, rest = pair_bias(bias_first, 0), pair_bias(bias_any, 0)
        return [first] * groups, [rest] * groups
    return ([pair_bias(bias_first, p) for p in range(groups)],
            [pair_bias(bias_any, p) for p in range(groups)])


def _band_mask_keys(max_dist, first_step, groups):
    kc = lax.broadcasted_iota(jnp.int32, (2 * BLOCK, BLOCK), 0)
    qi = lax.broadcasted_iota(jnp.int32, (2 * BLOCK, BLOCK), 1)
    dist = qi + BLOCK - kc
    band = (dist >= 0) & (dist <= max_dist)
    first_key = jnp.where(first_step, BLOCK, 0)
    rest = jnp.where(band, 0.0, MASKED).astype(jnp.bfloat16)
    first = jnp.where(band & (kc >= first_key), 0.0, MASKED).astype(jnp.bfloat16)
    row = lax.broadcasted_iota(jnp.int32, (2 * BLOCK, BLOCK), 0)
    onehot = jnp.where(row % BLOCK == qi, 1.0, 0.0).astype(jnp.bfloat16)
    return [(first, onehot)] * groups, [(rest, onehot)] * groups


def _band_chains(*args, **kwargs):
    for _ in _band_chain_steps(*args, **kwargs):
        pass


def _interleave(*step_iters):
    iters = [iter(s) for s in step_iters]
    while iters:
        iters = [it for it in iters if next(it, StopIteration) is not StopIteration]


def _band_chain_steps(q_ref, kp_ref, ko_ref, vp_ref, vo_ref, bias_0, bias_rest, zero_row0, emit,
                      blocks=None):
    rows = q_ref.shape[2]
    groups = q_ref.shape[3] // LANES
    ones = jnp.ones((2 * BLOCK, LANES), jnp.bfloat16)
    tile = 16
    row0 = lax.broadcasted_iota(jnp.int32, (tile, 1), 0) == 0
    zero_first = lambda t: jnp.concatenate(
        [jnp.where(row0, jnp.zeros_like(t[:tile]), t[:tile]), t[tile:]], axis=0)
    for c in range(q_ref.shape[1]):
        for i in (range(rows // BLOCK) if blocks is None else blocks):
            r = slice(i * BLOCK, (i + 1) * BLOCK)
            q = q_ref[0, c, r, :]
            if i == 0:
                k_prev, v_prev = kp_ref[0, c], vp_ref[0, c]
            else:
                before = slice((i - 1) * BLOCK, i * BLOCK)
                k_prev, v_prev = ko_ref[0, c, before, :], vo_ref[0, c, before, :]
            if zero_row0:
                k_prev, v_prev = zero_first(k_prev), zero_first(v_prev)
            k = jnp.concatenate([k_prev, ko_ref[0, c, r, :]], axis=0)
            v = jnp.concatenate([v_prev, vo_ref[0, c, r, :]], axis=0)
            for p in range(groups):
                cols = slice(p * LANES, (p + 1) * LANES)
                v_ones = jnp.concatenate([v[:, cols], ones], axis=1)
                bias = bias_0[p] if i == 0 else bias_rest[p]
                in_matmul = isinstance(bias, tuple)
                emit(c, i, p, *_pair_attention(q[:, cols], k[:, cols], v_ones,
                                               None if in_matmul else bias,
                                               bias if in_matmul else None))
                yield


def _qkv_specs(dil, rows, width):
    n = rows // dil
    own = pl.BlockSpec((1, dil, n, width), lambda bi, j: (bi, 0, j, 0))
    prev = pl.BlockSpec(
        (1, dil, BLOCK, width), lambda bi, j: (bi, 0, jnp.maximum(j * (n // BLOCK) - 1, 0), 0))
    return [own, prev, own, prev, own]


def _attn_a_out_kernel(sink_ref, q_ref, kp_ref, ko_ref, vp_ref, vo_ref, gate_ref,
                       yb_ref, yc_ref, x_ref, w_ref, g_ref, out_ref, wb_ref):
    @pl.when((pl.program_id(0) == 0) & (pl.program_id(1) == 0))
    def _():
        wb_ref[...] = w_ref[...].astype(jnp.bfloat16)

    groups = q_ref.shape[3] // LANES
    bias_0, bias_rest = _band_biases(A_WINDOW - 1, pl.program_id(1) == 0, sink_ref, groups)
    tiles = {}

    def emit(c, i, p, acc, l, m):
        r = slice(i * BLOCK, (i + 1) * BLOCK)
        cols = slice(p * LANES, (p + 1) * LANES)
        gate = gate_ref[0, r, cols].astype(jnp.float32)
        tiles[i, p] = (acc * (1.0 / l) * gate).astype(jnp.bfloat16)

    _band_chains(q_ref, kp_ref, ko_ref, vp_ref, vo_ref, bias_0, bias_rest, True, emit)
    per_group = OUT_GROUP // BLOCK
    for grp in range(q_ref.shape[2] // OUT_GROUP):
        blocks = range(grp * per_group, (grp + 1) * per_group)
        r = slice(grp * OUT_GROUP, (grp + 1) * OUT_GROUP)
        ya = jnp.concatenate(
            [jnp.concatenate([tiles[i, p] for p in range(groups)], axis=1) for i in blocks], axis=0)
        y = jnp.concatenate([ya, yb_ref[0, r, :], yc_ref[0, r, :]], axis=1)
        z = jnp.dot(y, wb_ref[...], preferred_element_type=jnp.float32)
        out_ref[0, r, :] = x_ref[0, r, :] + z * _rms_scale(z) * g_ref[...]


def _attn_a_out(q, k, v, gate, sink, yb, yc, x, w, g):
    b, _, s, wq = q.shape
    d = x.shape[2]
    rows = OUT_ROWS
    row = lambda width: pl.BlockSpec((1, rows, width), lambda bi, j: (bi, j, 0))
    return pl.pallas_call(
        _attn_a_out_kernel,
        grid=(b, s // rows),
        in_specs=[pl.BlockSpec(memory_space=pltpu.SMEM)] + _qkv_specs(1, rows, wq)
        + [row(wq), row(yb.shape[2]), row(yc.shape[2]), row(d),
           pl.BlockSpec((d, d), lambda bi, j: (0, 0), pipeline_mode=pl.Buffered(1)),
           pl.BlockSpec((1, d), lambda bi, j: (0, 0))],
        out_specs=row(d),
        out_shape=jax.ShapeDtypeStruct((b, s, d), x.dtype),
        scratch_shapes=[pltpu.VMEM((d, d), jnp.bfloat16)],
        compiler_params=pltpu.CompilerParams(
            dimension_semantics=("arbitrary", "arbitrary"), vmem_limit_bytes=VMEM_LIMIT),
        name="attn_a_out",
    )(sink, q, k, k, v, v, gate, yb, yc, x, w, g)


def _stage_pitch(dil):
    pitch = dil if dil % 8 else dil + dil // 2
    assert pitch == dil or pitch % 8 == 0
    return pitch


def _stage_arrays(dil):
    return 3 if _stage_pitch(dil) == dil else 2


def _stage_sets(level):
    return 1 if level == 0 else 2


def _attn_b_kernel(*refs):
    n_cfg = len(B_CONFIGS)
    gate_ref, y_ref = refs[5 * n_cfg:5 * n_cfg + 2]
    stage = list(refs[5 * n_cfg + 2:])
    span = y_ref.shape[1]
    first_step = pl.program_id(1) == 0
    groups = B_W // LANES
    order = sorted(range(n_cfg), key=lambda c: -B_CONFIGS[c][1])
    assert B_CONFIGS[order[-1]][1] == 1
    qkv, biases, cover = {}, {}, {}
    for c in order:
        win, dil = B_CONFIGS[c]
        qkv[dil] = refs[5 * c:5 * c + 5]
        biases[dil] = _band_mask_keys(win // dil, first_step, groups)
        cover[dil] = dil * BLOCK
    dils = [B_CONFIGS[c][1] for c in order]
    assert len(dils) == 3
    staged = {dil: [tuple(stage.pop(0) for _ in range(_stage_arrays(dil)))
                    for _ in range(_stage_sets(level))]
              for level, dil in enumerate(dils[:-1])}

    def steps(level, lo):
        dil = dils[level]
        blk = lo // cover[dil]

        def stage_of(coarse):
            sets = staged[coarse]
            return sets[(lo // cover[coarse]) % len(sets)]

        def emit_staged(cls, i, p, acc, l, m):
            rows = pl.ds(cls, BLOCK, stride=_stage_pitch(dil))
            vals = (acc, l, m)
            if _stage_arrays(dil) == 2:
                vals = (acc * (1.0 / l), m + jnp.log(l) * LOG2E)
            for ref, val in zip(stage_of(dil), vals):
                ref[p, rows, :] = val

        def staged_rows(ref, coarse, p):
            pitch = _stage_pitch(coarse)
            run0 = (lo % cover[coarse]) // coarse
            if pitch == coarse:
                return ref[p, run0 * pitch:run0 * pitch + BLOCK, :]
            return jnp.concatenate(
                [ref[p, (run0 + g) * pitch:(run0 + g) * pitch + coarse, :]
                 for g in range(BLOCK // coarse)], axis=0)

        def emit_mixed(cls, i, p, acc, l, m):
            cols = slice(p * LANES, (p + 1) * LANES)
            accs, ls, ms = [acc], [l], [m]
            for coarse in dils[:-1]:
                vals = [staged_rows(ref, coarse, p) for ref in stage_of(coarse)]
                accs.append(vals[0])
                ls.append(vals[1] if len(vals) == 3 else None)
                ms.append(vals[-1])
            mx = functools.reduce(jnp.maximum, ms)
            wts = [jnp.exp2(mi - mx) for mi in ms]
            num = sum(w * a for w, a in zip(wts[1:], accs[1:])) + wts[0] * accs[0]
            den = sum((w if li is None else w * li) for w, li in zip(wts[1:], ls[1:])) + wts[0] * ls[0]
            r = slice(lo, lo + BLOCK)
            gate = gate_ref[0, r, cols].astype(jnp.float32)
            y_ref[0, r, cols] = (num * (1.0 / den) * gate).astype(y_ref.dtype)

        last = level == len(dils) - 1
        return _band_chain_steps(*qkv[dil], *biases[dil], False,
                                 emit_mixed if last else emit_staged, blocks=(blk,))

    coarse, middle, fine = dils
    for lo0 in range(0, span, cover[coarse]):
        for _ in steps(0, lo0):
            pass
        subs = list(range(lo0, lo0 + cover[coarse], cover[middle]))
        for _ in steps(1, subs[0]):
            pass
        for n, sub in enumerate(subs):
            mixing = itertools.chain.from_iterable(
                steps(2, lo) for lo in range(sub, sub + cover[middle], cover[fine]))
            ahead = steps(1, subs[n + 1]) if n + 1 < len(subs) else ()
            _interleave(mixing, ahead)


def _attn_b(qkv_b, gate):
    n_cfg = len(B_CONFIGS)
    b, _, s, wq = qkv_b[0].shape
    rows = ATTN_ROWS
    in_specs, args = [], []
    for c, (_, dil) in enumerate(B_CONFIGS):
        q, k, v = qkv_b[c], qkv_b[n_cfg + c], qkv_b[2 * n_cfg + c]
        in_specs += _qkv_specs(dil, rows, wq)
        args += [q, k, k, v, v]
    row = pl.BlockSpec((1, rows, wq), lambda bi, j: (bi, j, 0))
    stages = []
    for level, dil in enumerate(sorted((d for _, d in B_CONFIGS if d > 1), reverse=True)):
        assert rows % (dil * BLOCK) == 0
        stages += [pltpu.VMEM((wq // LANES, _stage_pitch(dil) * BLOCK, LANES),
                              jnp.float32)] * (_stage_arrays(dil) * _stage_sets(level))
    return pl.pallas_call(
        _attn_b_kernel,
        grid=(b, s // rows),
        in_specs=in_specs + [row],
        out_specs=row,
        out_shape=jax.ShapeDtypeStruct((b, s, wq), jnp.bfloat16),
        scratch_shapes=stages,
        compiler_params=pltpu.CompilerParams(
            dimension_semantics=("parallel", "parallel"), vmem_limit_bytes=ATTN_B_VMEM_LIMIT),
        name="attn_b",
    )(*args, gate)


def _rope_tables(seq):
    inv_freq = ROPE_THETA ** (-np.arange(0, ROT_DIM, 2, dtype=np.float64) / ROT_DIM)
    ang = np.arange(seq, dtype=np.float64)[:, None] * inv_freq[None, :]
    cos, sin = np.cos(ang), np.sin(ang)
    pad = HEAD_DIM - ROT_DIM
    cos_h = np.concatenate([cos, cos, np.ones((seq, pad))], axis=1)
    sin_h = np.concatenate([-sin, sin, np.zeros((seq, pad))], axis=1)
    reps = LANES // HEAD_DIM
    return (jnp.asarray(np.tile(cos_h, (1, reps)), jnp.float32),
            jnp.asarray(np.tile(sin_h, (1, reps)), jnp.float32))


def kernel(x, mem, pre_norm, w_in, sink_a, mem_norm, w_mem_kv, w_out, post_norm):
    depth = pre_norm.shape[0]
    cos_t, sin_t = _rope_tables(x.shape[1])
    h = x
    for l in range(depth):
        mk, mv = _mem_kv(mem, mem_norm[l][None], w_mem_kv[l])
        qa, ka, va, ga, gb, yc, *qkv_b = _in_proj(
            h, pre_norm[l][None], w_in[l], cos_t, sin_t, mk, mv)
        yb = _attn_b(qkv_b, gb)
        h = _attn_a_out(qa[:, None], ka[:, None], va[:, None], ga, sink_a[l], yb, yc, h,
                        w_out[l], post_norm[l][None])
    return h
```

```python
import functools
import itertools
import math

import jax
import jax.numpy as jnp
import numpy as np
from jax import lax
from jax.experimental import pallas as pl
from jax.experimental.pallas import tpu as pltpu

HEAD_DIM = 64
ROT_DIM = HEAD_DIM // 4
ROPE_THETA = 500000.0
BLOCK = 128
C_HEADS = 4
A_Q_HEADS = 6
A_KV_HEADS = 2
A_GROUP = A_Q_HEADS // A_KV_HEADS
A_WINDOW = 128
B_HEADS = 6
B_CONFIGS = ((128, 1), (512, 4), (2048, 16))
RMS_EPS = 1e-6

LANES = 128
A_W = A_Q_HEADS * HEAD_DIM
A_KV_W = A_KV_HEADS * HEAD_DIM
B_W = B_HEADS * HEAD_DIM
C_W = C_HEADS * HEAD_DIM

IN_ROWS = 1024
IN_PARTS = 1
ATTN_ROWS = 2048
OUT_ROWS = 1024
OUT_GROUP = 1024
VMEM_LIMIT = 48 * 1024 * 1024
ATTN_B_VMEM_LIMIT = 56 * 1024 * 1024
IN_VMEM_LIMIT = 60 * 1024 * 1024

MASKED = -3e38
LOG2E = math.log2(math.e)
Q_SCALE = HEAD_DIM ** -0.5 * LOG2E


def _rms_scale(xf):
    return lax.rsqrt(jnp.mean(xf * xf, axis=-1, keepdims=True) + RMS_EPS)


def _silu(g):
    return g * (1.0 / (1.0 + jnp.exp(-g)))


def _left_half():
    return lax.broadcasted_iota(jnp.int32, (1, LANES), 1) < HEAD_DIM


def _pair_attention(q2, kg, v_ones, bias, bias_keys=None):
    rows = q2.shape[0]
    left = _left_half()
    keep_l = jnp.where(left, 1.0, 0.0).astype(q2.dtype)
    qs = jnp.concatenate([q2 * keep_l, q2 * (1 - keep_l)], axis=0)
    if bias_keys is not None:
        mask_t, onehot = bias_keys
        qs = jnp.concatenate([qs, onehot], axis=1)
        kg = jnp.concatenate([kg, mask_t], axis=1)
    s = lax.dot_general(qs, kg, (((1,), (1,)), ((), ())), preferred_element_type=jnp.float32)
    if bias is not None:
        s = s + bias
    m = jnp.max(s, axis=-1, keepdims=True)
    e = jnp.exp2((s - m).astype(jnp.bfloat16))
    pv = jnp.dot(e, v_ones, preferred_element_type=jnp.float32)
    acc = jnp.where(left, pv[:rows, :LANES], pv[rows:, :LANES])
    l = jnp.where(left, pv[:rows, LANES:], pv[rows:, LANES:])
    m_own = jnp.where(left, m[:rows], m[rows:])
    return acc, l, m_own


def _mem_kv_kernel(mem_ref, g_ref, w_ref, mk_ref, mv_ref):
    b, n_mem, d = mem_ref.shape
    m = mem_ref[...].reshape(b * n_mem, d)
    u = (m * _rms_scale(m) * g_ref[...]).astype(jnp.bfloat16)
    kv = jnp.dot(u, w_ref[...].astype(jnp.bfloat16), preferred_element_type=jnp.float32)
    mk_ref[...] = kv[:, :C_W].astype(jnp.bfloat16).reshape(b, n_mem, C_W)
    mv_ref[...] = kv[:, C_W:].astype(jnp.bfloat16).reshape(b, n_mem, C_W)


def _mem_kv(mem, g, w):
    b, n_mem, d = mem.shape
    whole = lambda shape: pl.BlockSpec(shape, lambda i: (0,) * len(shape))
    return pl.pallas_call(
        _mem_kv_kernel,
        grid=(1,),
        in_specs=[whole((b, n_mem, d)), whole((1, d)), whole((d, 2 * C_W))],
        out_specs=[whole((b, n_mem, C_W)), whole((b, n_mem, C_W))],
        out_shape=[jax.ShapeDtypeStruct((b, n_mem, C_W), jnp.bfloat16)] * 2,
        name="mem_kv",
    )(mem, g, w)


_IN_WIDTHS = (("qa", A_W), ("ka", A_KV_W), ("va", A_KV_W), ("ga", A_W),
              ("qb", B_W), ("kb", B_W), ("vb", B_W), ("gb", B_W), ("qc", C_W), ("gc", C_W))
D_IN = sum(w for _, w in _IN_WIDTHS)
_DOT_GROUPS = (("qa", "ka"), ("va", "ga"), ("qb", "kb"), ("vb", "gb"), ("qc", "gc"))


def _rope(t, cos_t, sin_t, first8):
    half = ROT_DIM // 2
    out = []
    for j in range(t.shape[1] // LANES):
        tj = t[:, j * LANES:(j + 1) * LANES]
        partner = jnp.where(first8, pltpu.roll(tj, LANES - half, 1), pltpu.roll(tj, half, 1))
        out.append(tj * cos_t + partner * sin_t)
    return out[0] if len(out) == 1 else jnp.concatenate(out, axis=1)


def _expand_kv(t):
    left = _left_half()
    swapped = pltpu.roll(t, HEAD_DIM, 1)
    groups = []
    for p in range(A_Q_HEADS // 2):
        kv_l, kv_r = (2 * p) // A_GROUP, (2 * p + 1) // A_GROUP
        if (kv_l, kv_r) == (0, 1):
            groups.append(t)
        elif kv_l == kv_r == 0:
            groups.append(jnp.where(left, t, swapped))
        else:
            groups.append(jnp.where(left, swapped, t))
    return jnp.concatenate(groups, axis=1)


def _store_dilated(t, refs, part, stage_ref, stage4_ref):
    rows = t.shape[0]
    slabs = t.shape[1] // LANES
    refs[1][0, 0, part * rows:(part + 1) * rows, :] = t.astype(refs[1].dtype)
    for sl in range(slabs):
        stage_ref[sl] = t[:, sl * LANES:(sl + 1) * LANES]
    n4, n16 = rows // 4, rows // 16
    for sl in range(slabs):
        cols = slice(sl * LANES, (sl + 1) * LANES)
        for r in range(4):
            c4 = stage_ref[sl, pl.ds(r, n4, stride=4), :]
            refs[4][0, r, part * n4:(part + 1) * n4, cols] = c4.astype(refs[4].dtype)
            stage4_ref[sl, r * n4:(r + 1) * n4, :] = c4
        for r in range(4):
            for q in range(4):
                c16 = stage4_ref[sl, pl.ds(r * n4 + q, n16, stride=4), :]
                refs[16][0, r + 4 * q, part * n16:(part + 1) * n16, cols] = c16.astype(
                    refs[16].dtype)


def _in_proj_kernel(x_ref, g_ref, w_ref, cos_ref, sin_ref, mk_ref, mv_ref,
                    qa_ref, ka_ref, va_ref, ga_ref, gb_ref, yc_ref, *rest):
    dils = [d for _, d in B_CONFIGS]
    qb_refs = dict(zip(dils, rest[0:3]))
    kb_refs = dict(zip(dils, rest[3:6]))
    vb_refs = dict(zip(dils, rest[6:9]))
    wb_ref = rest[9]
    stages = rest[10:]
    bf16 = jnp.bfloat16

    @pl.when((pl.program_id(0) == 0) & (pl.program_id(1) == 0))
    def _():
        for c0 in range(0, D_IN, 4 * LANES):
            wb_ref[:, c0:c0 + 4 * LANES] = w_ref[:, c0:c0 + 4 * LANES].astype(bf16)

    lane = lax.broadcasted_iota(jnp.int32, (1, LANES), 1)
    first8 = (lane % HEAD_DIM) < (ROT_DIM // 2)
    offsets = {}
    off = 0
    for name, wd in _IN_WIDTHS:
        offsets[name] = (off, wd)
        off += wd
    ones = jnp.ones((mv_ref.shape[1], LANES), bf16)

    part_rows = x_ref.shape[1] // IN_PARTS
    for part in range(IN_PARTS):
        r = slice(part * part_rows, (part + 1) * part_rows)
        stage_ref, stage4_ref = stages[2 * part:2 * part + 2]
        xf = x_ref[0, r, :]
        u = (xf * _rms_scale(xf) * g_ref[...]).astype(bf16)
        cos_t = cos_ref[r, :]
        sin_t = sin_ref[r, :]
        rope = lambda t, cos_t=cos_t, sin_t=sin_t: _rope(t, cos_t, sin_t, first8)

        def proj(names, u=u):
            c0 = offsets[names[0]][0]
            c1 = offsets[names[-1]][0] + offsets[names[-1]][1]
            full = jnp.dot(u, wb_ref[:, c0:c1], preferred_element_type=jnp.float32)
            return [full[:, offsets[n][0] - c0:offsets[n][0] - c0 + offsets[n][1]] for n in names]

        qc, gc = proj(_DOT_GROUPS[4])
        qc = (qc * Q_SCALE).astype(bf16)
        oc = []
        for p in range(C_W // LANES):
            cols = slice(p * LANES, (p + 1) * LANES)
            v_ones = jnp.concatenate([mv_ref[0, :, cols], ones], axis=1)
            acc, l, _ = _pair_attention(qc[:, cols], mk_ref[0, :, cols], v_ones, None)
            oc.append(acc * (1.0 / l))
        yc_ref[0, r, :] = (jnp.concatenate(oc, axis=1) * _silu(gc)).astype(bf16)

        qb, kb = proj(_DOT_GROUPS[2])
        _store_dilated(rope(qb) * Q_SCALE, qb_refs, part, stage_ref, stage4_ref)
        _store_dilated(rope(kb), kb_refs, part, stage_ref, stage4_ref)
        vb, gb = proj(_DOT_GROUPS[3])
        _store_dilated(vb, vb_refs, part, stage_ref, stage4_ref)
        gb_ref[0, r, :] = _silu(gb).astype(bf16)
        qa, ka = proj(_DOT_GROUPS[0])
        qa_ref[0, r, :] = (rope(qa) * Q_SCALE).astype(bf16)
        ka_ref[0, r, :] = _expand_kv(rope(ka)).astype(bf16)
        va, ga = proj(_DOT_GROUPS[1])
        va_ref[0, r, :] = _expand_kv(va).astype(bf16)
        ga_ref[0, r, :] = _silu(ga).astype(bf16)


def _in_proj(x, g, w, cos_t, sin_t, mk, mv):
    b, s, d = x.shape
    tm = IN_ROWS
    n_mem = mk.shape[1]
    row = lambda width: pl.BlockSpec((1, tm, width), lambda i, bi: (bi, i, 0))
    out_widths = (A_W, A_W, A_W, A_W, B_W, C_W)
    out_specs = [row(wd) for wd in out_widths]
    out_shape = [jax.ShapeDtypeStruct((b, s, wd), jnp.bfloat16) for wd in out_widths]
    for _ in range(3):
        for _, dil in B_CONFIGS:
            out_specs.append(pl.BlockSpec((1, dil, tm // dil, B_W), lambda i, bi: (bi, 0, i, 0)))
            out_shape.append(jax.ShapeDtypeStruct((b, dil, s // dil, B_W), jnp.bfloat16))
    stage = pltpu.VMEM((B_W // LANES, tm // IN_PARTS, LANES), jnp.float32)
    return pl.pallas_call(
        _in_proj_kernel,
        grid=(s // tm, b),
        in_specs=[
            row(d),
            pl.BlockSpec((1, d), lambda i, bi: (0, 0)),
            pl.BlockSpec((d, D_IN), lambda i, bi: (0, 0), pipeline_mode=pl.Buffered(1)),
            pl.BlockSpec((tm, LANES), lambda i, bi: (i, 0)),
            pl.BlockSpec((tm, LANES), lambda i, bi: (i, 0)),
            pl.BlockSpec((1, n_mem, C_W), lambda i, bi: (bi, 0, 0)),
            pl.BlockSpec((1, n_mem, C_W), lambda i, bi: (bi, 0, 0)),
        ],
        out_specs=out_specs,
        out_shape=out_shape,
        scratch_shapes=[pltpu.VMEM((d, D_IN), jnp.bfloat16)] + [stage] * (2 * IN_PARTS),
        compiler_params=pltpu.CompilerParams(
            dimension_semantics=("arbitrary", "arbitrary"), vmem_limit_bytes=IN_VMEM_LIMIT),
        name="in_proj",
    )(x, g, w, cos_t, sin_t, mk, mv)


def _band_biases(max_dist, first_step, sink_ref, groups):
    qi = lax.broadcasted_iota(jnp.int32, (BLOCK, 2 * BLOCK), 0)
    kc = lax.broadcasted_iota(jnp.int32, (BLOCK, 2 * BLOCK), 1)
    dist = qi + BLOCK - kc
    band = (dist >= 0) & (dist <= max_dist)
    first_key = jnp.where(first_step, BLOCK, 0)
    neg = jnp.float32(-jnp.inf)
    bias_any = jnp.where(band, 0.0, neg)
    bias_first = jnp.where(band & (kc >= first_key), 0.0, neg)

    def pair_bias(base, p):
        if sink_ref is None:
            return jnp.concatenate([base, base], axis=0)
        assert max_dist < BLOCK
        halves = [jnp.where(kc == 0, sink_ref[2 * p + h] * LOG2E, base) for h in range(2)]
        return jnp.concatenate(halves, axis=0)

    if sink_ref is None:
        first, rest = pair_bias(bias_first, 0), pair_bias(bias_any, 0)
        return [first] * groups, [rest] * groups
    return ([pair_bias(bias_first, p) for p in range(groups)],
            [pair_bias(bias_any, p) for p in range(groups)])


def _band_mask_keys(max_dist, first_step, groups):
    kc = lax.broadcasted_iota(jnp.int32, (2 * BLOCK, BLOCK), 0)
    qi = lax.broadcasted_iota(jnp.int32, (2 * BLOCK, BLOCK), 1)
    dist = qi + BLOCK - kc
    band = (dist >= 0) & (dist <= max_dist)
    first_key = jnp.where(first_step, BLOCK, 0)
    rest = jnp.where(band, 0.0, MASKED).astype(jnp.bfloat16)
    first = jnp.where(band & (kc >= first_key), 0.0, MASKED).astype(jnp.bfloat16)
    row = lax.broadcasted_iota(jnp.int32, (2 * BLOCK, BLOCK), 0)
    onehot = jnp.where(row % BLOCK == qi, 1.0, 0.0).astype(jnp.bfloat16)
    return [(first, onehot)] * groups, [(rest, onehot)] * groups


def _band_chains(*args, **kwargs):
    for _ in _band_chain_steps(*args, **kwargs):
        pass


def _interleave(*step_iters):
    iters = [iter(s) for s in step_iters]
    while iters:
        iters = [it for it in iters if next(it, StopIteration) is not StopIteration]


def _band_chain_steps(q_ref, kp_ref, ko_ref, vp_ref, vo_ref, bias_0, bias_rest, zero_row0, emit,
                      blocks=None):
    rows = q_ref.shape[2]
    groups = q_ref.shape[3] // LANES
    ones = jnp.ones((2 * BLOCK, LANES), jnp.bfloat16)
    tile = 16
    row0 = lax.broadcasted_iota(jnp.int32, (tile, 1), 0) == 0
    zero_first = lambda t: jnp.concatenate(
        [jnp.where(row0, jnp.zeros_like(t[:tile]), t[:tile]), t[tile:]], axis=0)
    for c in range(q_ref.shape[1]):
        for i in (range(rows // BLOCK) if blocks is None else blocks):
            r = slice(i * BLOCK, (i + 1) * BLOCK)
            q = q_ref[0, c, r, :]
            if i == 0:
                k_prev, v_prev = kp_ref[0, c], vp_ref[0, c]
            else:
                before = slice((i - 1) * BLOCK, i * BLOCK)
                k_prev, v_prev = ko_ref[0, c, before, :], vo_ref[0, c, before, :]
            if zero_row0:
                k_prev, v_prev = zero_first(k_prev), zero_first(v_prev)
            k = jnp.concatenate([k_prev, ko_ref[0, c, r, :]], axis=0)
            v = jnp.concatenate([v_prev, vo_ref[0, c, r, :]], axis=0)
            for p in range(groups):
                cols = slice(p * LANES, (p + 1) * LANES)
                v_ones = jnp.concatenate([v[:, cols], ones], axis=1)
                bias = bias_0[p] if i == 0 else bias_rest[p]
                in_matmul = isinstance(bias, tuple)
                emit(c, i, p, *_pair_attention(q[:, cols], k[:, cols], v_ones,
                                               None if in_matmul else bias,
                                               bias if in_matmul else None))
                yield


def _qkv_specs(dil, rows, width):
    n = rows // dil
    own = pl.BlockSpec((1, dil, n, width), lambda bi, j: (bi, 0, j, 0))
    prev = pl.BlockSpec(
        (1, dil, BLOCK, width), lambda bi, j: (bi, 0, jnp.maximum(j * (n // BLOCK) - 1, 0), 0))
    return [own, prev, own, prev, own]


def _attn_a_out_kernel(sink_ref, q_ref, kp_ref, ko_ref, vp_ref, vo_ref, gate_ref,
                       yb_ref, yc_ref, x_ref, w_ref, g_ref, out_ref, wb_ref):
    @pl.when((pl.program_id(0) == 0) & (pl.program_id(1) == 0))
    def _():
        wb_ref[...] = w_ref[...].astype(jnp.bfloat16)

    groups = q_ref.shape[3] // LANES
    bias_0, bias_rest = _band_biases(A_WINDOW - 1, pl.program_id(1) == 0, sink_ref, groups)
    tiles = {}

    def emit(c, i, p, acc, l, m):
        r = slice(i * BLOCK, (i + 1) * BLOCK)
        cols = slice(p * LANES, (p + 1) * LANES)
        gate = gate_ref[0, r, cols].astype(jnp.float32)
        tiles[i, p] = (acc * (1.0 / l) * gate).astype(jnp.bfloat16)

    per_group = OUT_GROUP // BLOCK
    for grp in range(q_ref.shape[2] // OUT_GROUP):
        blocks = range(grp * per_group, (grp + 1) * per_group)
        _band_chains(q_ref, kp_ref, ko_ref, vp_ref, vo_ref, bias_0, bias_rest, True, emit,
                     blocks=blocks)
        r = slice(grp * OUT_GROUP, (grp + 1) * OUT_GROUP)
        ya = jnp.concatenate(
            [jnp.concatenate([tiles[i, p] for p in range(groups)], axis=1) for i in blocks], axis=0)
        y = jnp.concatenate([ya, yb_ref[0, r, :], yc_ref[0, r, :]], axis=1)
        z = jnp.dot(y, wb_ref[...], preferred_element_type=jnp.float32)
        out_ref[0, r, :] = x_ref[0, r, :] + z * _rms_scale(z) * g_ref[...]


def _attn_a_out(q, k, v, gate, sink, yb, yc, x, w, g):
    b, _, s, wq = q.shape
    d = x.shape[2]
    rows = OUT_ROWS
    row = lambda width: pl.BlockSpec((1, rows, width), lambda bi, j: (bi, j, 0))
    return pl.pallas_call(
        _attn_a_out_kernel,
        grid=(b, s // rows),
        in_specs=[pl.BlockSpec(memory_space=pltpu.SMEM)] + _qkv_specs(1, rows, wq)
        + [row(wq), row(yb.shape[2]), row(yc.shape[2]), row(d),
           pl.BlockSpec((d, d), lambda bi, j: (0, 0), pipeline_mode=pl.Buffered(1)),
           pl.BlockSpec((1, d), lambda bi, j: (0, 0))],
        out_specs=row(d),
        out_shape=jax.ShapeDtypeStruct((b, s, d), x.dtype),
        scratch_shapes=[pltpu.VMEM((d, d), jnp.bfloat16)],
        compiler_params=pltpu.CompilerParams(
            dimension_semantics=("arbitrary", "arbitrary"), vmem_limit_bytes=VMEM_LIMIT),
        name="attn_a_out",
    )(sink, q, k, k, v, v, gate, yb, yc, x, w, g)


def _stage_pitch(dil):
    pitch = dil if dil % 8 else dil + dil // 2
    assert pitch == dil or pitch % 8 == 0
    return pitch


def _stage_arrays(dil):
    return 3 if _stage_pitch(dil) == dil else 2


def _stage_sets(level):
    return 1 if level == 0 else 2


def _attn_b_kernel(*refs):
    n_cfg = len(B_CONFIGS)
    gate_ref, y_ref = refs[5 * n_cfg:5 * n_cfg + 2]
    stage = list(refs[5 * n_cfg + 2:])
    span = y_ref.shape[1]
    first_step = pl.program_id(1) == 0
    groups = B_W // LANES
    order = sorted(range(n_cfg), key=lambda c: -B_CONFIGS[c][1])
    assert B_CONFIGS[order[-1]][1] == 1
    qkv, biases, cover = {}, {}, {}
    for c in order:
        win, dil = B_CONFIGS[c]
        qkv[dil] = refs[5 * c:5 * c + 5]
        biases[dil] = _band_mask_keys(win // dil, first_step, groups)
        cover[dil] = dil * BLOCK
    dils = [B_CONFIGS[c][1] for c in order]
    assert len(dils) == 3
    staged = {dil: [tuple(stage.pop(0) for _ in range(_stage_arrays(dil)))
                    for _ in range(_stage_sets(level))]
              for level, dil in enumerate(dils[:-1])}

    def steps(level, lo):
        dil = dils[level]
        blk = lo // cover[dil]

        def stage_of(coarse):
            sets = staged[coarse]
            return sets[(lo // cover[coarse]) % len(sets)]

        def emit_staged(cls, i, p, acc, l, m):
            rows = pl.ds(cls, BLOCK, stride=_stage_pitch(dil))
            vals = (acc, l, m)
            if _stage_arrays(dil) == 2:
                vals = (acc * (1.0 / l), m + jnp.log(l) * LOG2E)
            for ref, val in zip(stage_of(dil), vals):
                ref[p, rows, :] = val

        def staged_rows(ref, coarse, p):
            pitch = _stage_pitch(coarse)
            run0 = (lo % cover[coarse]) // coarse
            if pitch == coarse:
                return ref[p, run0 * pitch:run0 * pitch + BLOCK, :]
            return jnp.concatenate(
                [ref[p, (run0 + g) * pitch:(run0 + g) * pitch + coarse, :]
                 for g in range(BLOCK // coarse)], axis=0)

        def emit_mixed(cls, i, p, acc, l, m):
            cols = slice(p * LANES, (p + 1) * LANES)
            accs, ls, ms = [acc], [l], [m]
            for coarse in dils[:-1]:
                vals = [staged_rows(ref, coarse, p) for ref in stage_of(coarse)]
                accs.append(vals[0])
                ls.append(vals[1] if len(vals) == 3 else None)
                ms.append(vals[-1])
            mx = functools.reduce(jnp.maximum, ms)
            wts = [jnp.exp2(mi - mx) for mi in ms]
            num = sum(w * a for w, a in zip(wts[1:], accs[1:])) + wts[0] * accs[0]
            den = sum((w if li is None else w * li) for w, li in zip(wts[1:], ls[1:])) + wts[0] * ls[0]
            r = slice(lo, lo + BLOCK)
            gate = gate_ref[0, r, cols].astype(jnp.float32)
            y_ref[0, r, cols] = (num * (1.0 / den) * gate).astype(y_ref.dtype)

        last = level == len(dils) - 1
        return _band_chain_steps(*qkv[dil], *biases[dil], False,
                                 emit_mixed if last else emit_staged, blocks=(blk,))

    coarse, middle, fine = dils
    for lo0 in range(0, span, cover[coarse]):
        subs = list(range(lo0, lo0 + cover[coarse], cover[middle]))
        _interleave(steps(0, lo0), steps(1, subs[0]))
        for n, sub in enumerate(subs):
            mixing = itertools.chain.from_iterable(
                steps(2, lo) for lo in range(sub, sub + cover[middle], cover[fine]))
            ahead = steps(1, subs[n + 1]) if n + 1 < len(subs) else ()
            _interleave(mixing, ahead)


def _attn_b(qkv_b, gate):
    n_cfg = len(B_CONFIGS)
    b, _, s, wq = qkv_b[0].shape
    rows = ATTN_ROWS
    in_specs, args = [], []
    for c, (_, dil) in enumerate(B_CONFIGS):
        q, k, v = qkv_b[c], qkv_b[n_cfg + c], qkv_b[2 * n_cfg + c]
        in_specs += _qkv_specs(dil, rows, wq)
        args += [q, k, k, v, v]
    row = pl.BlockSpec((1, rows, wq), lambda bi, j: (bi, j, 0))
    stages = []
    for level, dil in enumerate(sorted((d for _, d in B_CONFIGS if d > 1), reverse=True)):
        assert rows % (dil * BLOCK) == 0
        stages += [pltpu.VMEM((wq // LANES, _stage_pitch(dil) * BLOCK, LANES),
                              jnp.float32)] * (_stage_arrays(dil) * _stage_sets(level))
    return pl.pallas_call(
        _attn_b_kernel,
        grid=(b, s // rows),
        in_specs=in_specs + [row],
        out_specs=row,
        out_shape=jax.ShapeDtypeStruct((b, s, wq), jnp.bfloat16),
        scratch_shapes=stages,
        compiler_params=pltpu.CompilerParams(
            dimension_semantics=("parallel", "parallel"), vmem_limit_bytes=ATTN_B_VMEM_LIMIT),
        name="attn_b",
    )(*args, gate)


def _rope_tables(seq):
    inv_freq = ROPE_THETA ** (-np.arange(0, ROT_DIM, 2, dtype=np.float64) / ROT_DIM)
    ang = np.arange(seq, dtype=np.float64)[:, None] * inv_freq[None, :]
    cos, sin = np.cos(ang), np.sin(ang)
    pad = HEAD_DIM - ROT_DIM
    cos_h = np.concatenate([cos, cos, np.ones((seq, pad))], axis=1)
    sin_h = np.concatenate([-sin, sin, np.zeros((seq, pad))], axis=1)
    reps = LANES // HEAD_DIM
    return (jnp.asarray(np.tile(cos_h, (1, reps)), jnp.float32),
            jnp.asarray(np.tile(sin_h, (1, reps)), jnp.float32))


def kernel(x, mem, pre_norm, w_in, sink_a, mem_norm, w_mem_kv, w_out, post_norm):
    depth = pre_norm.shape[0]
    cos_t, sin_t = _rope_tables(x.shape[1])
    h = x
    for l in range(depth):
        mk, mv = _mem_kv(mem, mem_norm[l][None], w_mem_kv[l])
        qa, ka, va, ga, gb, yc, *qkv_b = _in_proj(
            h, pre_norm[l][None], w_in[l], cos_t, sin_t, mk, mv)
        yb = _attn_b(qkv_b, gb)
        h = _attn_a_out(qa[:, None], ka[:, None], va[:, None], ga, sink_a[l], yb, yc, h,
                        w_out[l], post_norm[l][None])
    return h
```
